```python
import jax, jax.numpy as jnp
from jax import lax
import numpy as np

D_MODEL = 1024
BATCH = 8
SEQ = 4096
DEPTH = 2

N_A_LAYERS = DEPTH // 2
N_B_LAYERS = DEPTH - N_A_LAYERS
HGRN_EXPAND = 128
HGRN_HEADS = D_MODEL // HGRN_EXPAND
HGRN_DK = HGRN_EXPAND
HGRN_DV = D_MODEL // HGRN_HEADS
HGRN_CHUNK = 64
MLA_HEADS = 16
MLA_NOPE = 128
MLA_ROPE = 64
MLA_V = 128
MLA_Q_LORA = 256
MLA_KV_LORA = 256
ROPE_THETA = 10000.0
QBLOCK = 128
D_FF = 4 * D_MODEL
EPS = 1e-6

kernel_name = 'hybrid_hgrn2_mla_yoco'


def rmsnorm(x, gain):
    xf = x.astype(jnp.float32)
    y = xf * lax.rsqrt(jnp.mean(xf * xf, axis=-1, keepdims=True) + EPS)
    return (y * gain.astype(jnp.float32)).astype(x.dtype)


def rope_tables(seq):
    half = MLA_ROPE // 2
    inv_freq = ROPE_THETA ** (-jnp.arange(half, dtype=jnp.float32) / half)
    ang = jnp.arange(seq, dtype=jnp.float32)[:, None] * inv_freq[None, :]
    return jnp.cos(ang), jnp.sin(ang)


def apply_rope(x, cos, sin):
    half = MLA_ROPE // 2
    xf = x.astype(jnp.float32)
    x1, x2 = xf[..., :half], xf[..., half:]
    return jnp.concatenate([x1 * cos - x2 * sin, x2 * cos + x1 * sin], axis=-1).astype(x.dtype)


def hgrn_lower_bounds(lb_logits):
    return jnp.cumsum(jax.nn.softmax(lb_logits.astype(jnp.float32), axis=0), axis=0)


def hgrn2_mixer(xn, w_q, w_f, w_i, w_g, g_norm, w_o, lb):
    bsz, seq, _ = xn.shape
    nc = seq // HGRN_CHUNK
    f32 = jnp.float32
    q = jax.nn.silu((xn @ w_q).astype(f32))
    forget = lb + (1.0 - lb) * jax.nn.sigmoid((xn @ w_f).astype(f32))
    log_f = jnp.log(forget)
    k = 1.0 - forget
    v = (xn @ w_i).astype(f32)

    def chunks(t, d):
        return t.reshape(bsz, nc, HGRN_CHUNK, HGRN_HEADS, d).transpose(1, 0, 3, 2, 4)

    causal = jnp.tril(jnp.ones((HGRN_CHUNK, HGRN_CHUNK), dtype=bool))

    def step(state, inp):
        qc, kc, vc, gc = inp
        b = jnp.cumsum(gc, axis=2)
        o_inter = jnp.einsum('bhtd,bhdv->bhtv', qc * jnp.exp(b), state)
        diff = b[:, :, :, None, :] - b[:, :, None, :, :]
        decay = jnp.exp(jnp.where(causal[:, :, None], diff, -jnp.inf))
        scores = jnp.einsum('bhtd,bhsd,bhtsd->bhts', qc, kc, decay)
        o_intra = jnp.einsum('bhts,bhsv->bhtv', scores, vc)
        b_last = b[:, :, -1:, :]
        new_state = jnp.exp(b_last[:, :, 0, :])[..., None] * state + jnp.einsum(
            'bhsd,bhsv->bhdv', kc * jnp.exp(b_last - b), vc)
        return new_state, o_inter + o_intra

    state0 = jnp.zeros((bsz, HGRN_HEADS, HGRN_DK, HGRN_DV), f32)
    _, o = lax.scan(step, state0, (chunks(q, HGRN_DK), chunks(k, HGRN_DK),
                                   chunks(v, HGRN_DV), chunks(log_f, HGRN_DK)))
    o = o.transpose(1, 0, 3, 2, 4).reshape(bsz, seq, HGRN_HEADS, HGRN_DV)
    o = rmsnorm(o, g_norm)
    gate = jax.nn.silu((xn @ w_g).astype(f32)).reshape(bsz, seq, HGRN_HEADS, HGRN_DV)
    o = (o * gate).reshape(bsz, seq, D_MODEL).astype(xn.dtype)
    return o @ w_o


def shared_mla_kv(h, in_norm, w_dkv, kv_norm, w_uk, w_uv, cos, sin):
    bsz, seq, _ = h.shape
    hn = rmsnorm(h, in_norm)
    ckr = hn @ w_dkv
    c_kv = rmsnorm(ckr[..., :MLA_KV_LORA], kv_norm)
    k_rope = apply_rope(ckr[..., MLA_KV_LORA:], cos, sin)
    k_nope = (c_kv @ w_uk).reshape(bsz, seq, MLA_HEADS, MLA_NOPE)
    v = (c_kv @ w_uv).reshape(bsz, seq, MLA_HEADS, MLA_V)
    return k_nope, k_rope, v


def mla_mixer(xn, w_dq, q_norm, w_uq, w_o, k_nope, k_rope, v, cos, sin):
    bsz, seq, _ = xn.shape
    nb = seq // QBLOCK
    c_q = rmsnorm(xn @ w_dq, q_norm)
    q = (c_q @ w_uq).reshape(bsz, seq, MLA_HEADS, MLA_NOPE + MLA_ROPE)
    q_nope = q[..., :MLA_NOPE]
    q_rope = apply_rope(q[..., MLA_NOPE:], cos[:, None, :], sin[:, None, :])
    qn_b = q_nope.reshape(bsz, nb, QBLOCK, MLA_HEADS, MLA_NOPE).transpose(1, 0, 2, 3, 4)
    qr_b = q_rope.reshape(bsz, nb, QBLOCK, MLA_HEADS, MLA_ROPE).transpose(1, 0, 2, 3, 4)
    starts = jnp.arange(nb, dtype=jnp.int32) * QBLOCK
    key_pos = jnp.arange(seq, dtype=jnp.int32)
    scale = (MLA_NOPE + MLA_ROPE) ** -0.5

    def block(args):
        qn, qr, start = args
        s = jnp.einsum('bqhd,bkhd->bhqk', qn, k_nope) + jnp.einsum('bqhr,bkr->bhqk', qr, k_rope)
        s = s.astype(jnp.float32) * scale
        q_pos = start + jnp.arange(QBLOCK, dtype=jnp.int32)
        s = jnp.where(key_pos[None, :] <= q_pos[:, None], s, -jnp.inf)
        p = jax.nn.softmax(s, axis=-1).astype(v.dtype)
        return jnp.einsum('bhqk,bkhv->bqhv', p, v)

    o = lax.map(block, (qn_b, qr_b, starts))
    o = o.transpose(1, 0, 2, 3, 4).reshape(bsz, seq, MLA_HEADS * MLA_V)
    return o @ w_o


def sq_relu_mlp(xn, w_up, w_down):
    return jnp.square(jax.nn.relu(xn @ w_up)) @ w_down


def setup_inputs(seed: int = 0) -> dict:
    key = jax.random.key(seed)
    ks = jax.random.split(key, 24)
    f32 = jnp.float32

    def w(k, shape, fan_in):
        return jax.random.normal(k, shape, f32) * (fan_in ** -0.5)

    def gain(k, shape):
        return 1.0 + 0.02 * jax.random.normal(k, shape, f32)

    D = D_MODEL
    return {
        'x': jax.random.normal(ks[0], (BATCH, SEQ, D), f32),
        'hgrn_norm': gain(ks[1], (N_A_LAYERS, D)),
        'hgrn_w_q': w(ks[2], (N_A_LAYERS, D, D), D),
        'hgrn_w_f': w(ks[3], (N_A_LAYERS, D, D), D),
        'hgrn_w_i': w(ks[4], (N_A_LAYERS, D, D), D),
        'hgrn_w_g': w(ks[5], (N_A_LAYERS, D, D), D),
        'hgrn_g_norm': gain(ks[6], (N_A_LAYERS, HGRN_DV)),
        'hgrn_w_o': w(ks[7], (N_A_LAYERS, D, D), D),
        'hgrn_lb_logits': 0.5 * jax.random.normal(ks[8], (N_A_LAYERS + 1, D), f32),
        'mla_norm': gain(ks[9], (N_B_LAYERS, D)),
        'mla_w_dq': w(ks[10], (N_B_LAYERS, D, MLA_Q_LORA), D),
        'mla_q_norm': gain(ks[11], (N_B_LAYERS, MLA_Q_LORA)),
        'mla_w_uq': w(ks[12], (N_B_LAYERS, MLA_Q_LORA, MLA_HEADS * (MLA_NOPE + MLA_ROPE)), MLA_Q_LORA),
        'mla_w_o': w(ks[13], (N_B_LAYERS, MLA_HEADS * MLA_V, D), MLA_HEADS * MLA_V),
        'kv_in_norm': gain(ks[14], (D,)),
        'kv_w_dkv': w(ks[15], (D, MLA_KV_LORA + MLA_ROPE), D),
        'kv_norm': gain(ks[16], (MLA_KV_LORA,)),
        'kv_w_uk': w(ks[17], (MLA_KV_LORA, MLA_HEADS * MLA_NOPE), MLA_KV_LORA),
        'kv_w_uv': w(ks[18], (MLA_KV_LORA, MLA_HEADS * MLA_V), MLA_KV_LORA),
        'mlp_norm': gain(ks[19], (DEPTH, D)),
        'mlp_w_up': w(ks[20], (DEPTH, D, D_FF), D),
        'mlp_w_down': w(ks[21], (DEPTH, D_FF, D), D_FF),
        'final_norm': gain(ks[22], (D,)),
    }


def reference(x, hgrn_norm, hgrn_w_q, hgrn_w_f, hgrn_w_i, hgrn_w_g, hgrn_g_norm, hgrn_w_o,
              hgrn_lb_logits, mla_norm, mla_w_dq, mla_q_norm, mla_w_uq, mla_w_o,
              kv_in_norm, kv_w_dkv, kv_norm, kv_w_uk, kv_w_uv,
              mlp_norm, mlp_w_up, mlp_w_down, final_norm):
    seq = x.shape[1]
    cos, sin = rope_tables(seq)
    lower_bounds = hgrn_lower_bounds(hgrn_lb_logits)
    h = x
    k_nope = k_rope = v = None
    for l in range(DEPTH):
        if l < N_A_LAYERS:
            h = h + hgrn2_mixer(rmsnorm(h, hgrn_norm[l]), hgrn_w_q[l], hgrn_w_f[l], hgrn_w_i[l],
                                hgrn_w_g[l], hgrn_g_norm[l], hgrn_w_o[l], lower_bounds[l])
        else:
            j = l - N_A_LAYERS
            h = h + mla_mixer(rmsnorm(h, mla_norm[j]), mla_w_dq[j], mla_q_norm[j], mla_w_uq[j],
                              mla_w_o[j], k_nope, k_rope, v, cos, sin)
        h = h + sq_relu_mlp(rmsnorm(h, mlp_norm[l]), mlp_w_up[l], mlp_w_down[l])
        if l == N_A_LAYERS - 1:
            k_nope, k_rope, v = shared_mla_kv(h, kv_in_norm, kv_w_dkv, kv_norm, kv_w_uk, kv_w_uv, cos, sin)
    return rmsnorm(h, final_norm)
```

```python
import functools

import jax
import jax.numpy as jnp
from jax import lax
from jax.experimental import pallas as pl
from jax.experimental.pallas import tpu as pltpu

F32 = jnp.float32
BF16 = jnp.bfloat16

D_MODEL = 1024
HGRN_HEADS = 8
HGRN_DK = 128
MLA_HEADS = 16
MLA_NOPE = 128
MLA_ROPE = 64
MLA_V = 128
MLA_Q_LORA = 256
MLA_KV_LORA = 256
ROPE_THETA = 10000.0
D_FF = 4 * D_MODEL
EPS = 1e-6

LANES = 128
VMEM_LIMIT_BYTES = 56 * 1024 * 1024

SCAN_CHUNK = 64


def _tiles():
    return dict(proj=512, scan=512, mlp=512, mla_proj=512, attn_q=512, attn_k=512, ff=1024)


def _rms(x, gain):
    return x * lax.rsqrt(jnp.mean(x * x, axis=-1, keepdims=True) + EPS) * gain


def _sigmoid(x):
    return 1.0 / (1.0 + jnp.exp(-x))


def _dot(a, b):
    return jnp.dot(a, b, preferred_element_type=F32)


def _dot_nt(a, b):
    return lax.dot_general(a, b, (((1,), (1,)), ((), ())), preferred_element_type=F32)


def _dot_tn(a, b):
    return lax.dot_general(a, b, (((0,), (0,)), ((), ())), preferred_element_type=F32)


def _resident(shape):
    nd = len(shape)
    return pl.BlockSpec(shape, lambda *_: (0,) * nd, pipeline_mode=pl.Buffered(1))


def _params(semantics):
    return pltpu.CompilerParams(dimension_semantics=semantics, vmem_limit_bytes=VMEM_LIMIT_BYTES)


def _hgrn_proj_kernel(x_ref, gain_ref, lb_ref, w_ref, q_ref, f_ref, v_ref, g_ref):
    xb = _rms(x_ref[...], gain_ref[...]).astype(BF16)
    yq = _dot(xb, w_ref[0])
    q_ref[...] = (yq * _sigmoid(yq)).astype(BF16)
    logits = lb_ref[...]
    e = jnp.exp(logits - jnp.max(logits, axis=0, keepdims=True))
    lb = e[0:1, :] / jnp.sum(e, axis=0, keepdims=True)
    f_ref[...] = lb + (1.0 - lb) * _sigmoid(_dot(xb, w_ref[1]))
    v_ref[...] = _dot(xb, w_ref[2]).astype(BF16)
    yg = _dot(xb, w_ref[3])
    g_ref[...] = (yg * _sigmoid(yg)).astype(BF16)


def _hgrn_proj(x2, gain, lb_logits, w4):
    n, d = x2.shape
    tm = _tiles()["proj"]
    tok = lambda i: (i, 0)
    return pl.pallas_call(
        _hgrn_proj_kernel,
        grid=(n // tm,),
        in_specs=[
            pl.BlockSpec((tm, d), tok),
            _resident((1, d)),
            _resident(lb_logits.shape),
            _resident(w4.shape),
        ],
        out_specs=[pl.BlockSpec((tm, d), tok)] * 4,
        out_shape=[
            jax.ShapeDtypeStruct((n, d), BF16),
            jax.ShapeDtypeStruct((n, d), F32),
            jax.ShapeDtypeStruct((n, d), BF16),
            jax.ShapeDtypeStruct((n, d), BF16),
        ],
        compiler_params=_params(("arbitrary",)),
        name="hgrn_proj",
    )(x2, gain, lb_logits, w4)


def _hgrn_scan_kernel(q_ref, f_ref, v_ref, g_ref, gn_ref, o_ref, st_ref):
    c = SCAN_CHUNK
    n_chunks = q_ref.shape[0] // c

    @pl.when(pl.program_id(2) == 0)
    def _():
        st_ref[...] = jnp.zeros_like(st_ref)

    row = lax.broadcasted_iota(jnp.int32, (c, LANES), 0)
    t_i = lax.broadcasted_iota(jnp.int32, (c, c), 0)
    s_i = lax.broadcasted_iota(jnp.int32, (c, c), 1)
    diff = t_i ^ s_i
    level = jnp.zeros((c, c), jnp.int32)
    h = 1
    while h < c:
        level = level + (diff >= h).astype(jnp.int32)
        h *= 2
    level = jnp.where(s_i <= t_i, level, -1)
    gn = gn_ref[...]

    def chunk(i, carry):
        r0 = pl.multiple_of(i * c, c)
        rows = pl.ds(r0, c)
        f = f_ref[rows, :]
        k = 1.0 - f
        q = q_ref[rows, :].astype(F32)
        vb = v_ref[rows, :]

        scores = jnp.where(level == 0, _dot_nt(q.astype(BF16), k.astype(BF16)), 0.0)
        g_pre = f
        h_suf = jnp.ones_like(f)
        tot = f
        h = 1
        lvl = 1
        while h < c:
            p = _dot_nt((q * g_pre).astype(BF16), (k * h_suf).astype(BF16))
            scores = jnp.where(level == lvl, p, scores)
            up = pltpu.roll(tot, h, axis=0)
            dn = pltpu.roll(tot, c - h, axis=0)
            upper = (row & h) != 0
            g_pre = g_pre * jnp.where(upper, up, 1.0)
            h_suf = h_suf * jnp.where(upper, 1.0, dn)
            tot = tot * jnp.where(upper, up, dn)
            h *= 2
            lvl += 1

        st = st_ref[...]
        o = _dot_nt((q * g_pre).astype(BF16), st.astype(BF16)) + _dot(scores.astype(BF16), vb)
        st_ref[...] = st * tot[0:1, :] + _dot_tn(vb, (k * h_suf).astype(BF16))

        on = _rms(o, gn)
        o_ref[rows, :] = (on * g_ref[rows, :].astype(F32)).astype(BF16)
        return carry

    lax.fori_loop(0, n_chunks, chunk, 0)


def _hgrn_scan(q, f, v, gate, g_norm, batch, seq):
    d = q.shape[-1]
    ts = _tiles()["scan"]
    q, f, v, gate = (a.reshape(batch, seq, d) for a in (q, f, v, gate))
    blk = pl.BlockSpec((None, ts, HGRN_DK), lambda b, h, s: (b, s, h))
    out = pl.pallas_call(
        _hgrn_scan_kernel,
        grid=(batch, HGRN_HEADS, seq // ts),
        in_specs=[blk, blk, blk, blk, _resident((1, HGRN_DK))],
        out_specs=blk,
        out_shape=jax.ShapeDtypeStruct((batch, seq, d), BF16),
        scratch_shapes=[pltpu.VMEM((HGRN_DK, HGRN_DK), F32)],
        compiler_params=_params(("arbitrary", "arbitrary", "arbitrary")),
        name="hgrn_scan",
    )(q, f, v, gate, g_norm)
    return out.reshape(batch * seq, d)


def _resid_mlp_kernel(h_ref, a_ref, wo_ref, gain_ref, wup_ref, wdn_ref, fin_ref, o_ref, *, final_norm):
    tf = _tiles()["ff"]
    h1 = h_ref[...] + _dot(a_ref[...], wo_ref[...])
    xb = _rms(h1, gain_ref[...]).astype(BF16)
    acc = h1
    for c0 in range(0, D_FF, tf):
        up = jnp.maximum(_dot(xb, wup_ref[:, c0:c0 + tf]), 0.0)
        acc = acc + _dot((up * up).astype(BF16), wdn_ref[c0:c0 + tf, :])
    if final_norm:
        acc = _rms(acc, fin_ref[...])
    o_ref[...] = acc


def _resid_mlp(h, a, wo, gain, wup, wdn, fin, final_norm):
    n, d = h.shape
    ka = a.shape[1]
    tm = _tiles()["mlp"]
    tok = lambda i: (i, 0)
    return pl.pallas_call(
        functools.partial(_resid_mlp_kernel, final_norm=final_norm),
        grid=(n // tm,),
        in_specs=[
            pl.BlockSpec((tm, d), tok),
            pl.BlockSpec((tm, ka), tok),
            _resident(wo.shape),
            _resident((1, d)),
            _resident(wup.shape),
            _resident(wdn.shape),
            _resident((1, d)),
        ],
        out_specs=pl.BlockSpec((tm, d), tok),
        out_shape=jax.ShapeDtypeStruct((n, d), F32),
        compiler_params=_params(("arbitrary",)),
        name="resid_mlp_final" if final_norm else "resid_mlp",
    )(h, a, wo, gain, wup, wdn, fin)


def _mla_proj_kernel(h_ref, cs_ref, gkv_ref, wdkv_ref, gc_ref, wuk_ref, wuv_ref,
                     gq_ref, wdq_ref, gcq_ref, wuqn_ref, wuqr_ref,
                     qn_ref, qr_ref, kn_ref, kr_ref, v_ref, *, scale):
    h = h_ref[...]
    hn = h * lax.rsqrt(jnp.mean(h * h, axis=-1, keepdims=True) + EPS)
    cs = cs_ref[...]
    lane = lax.broadcasted_iota(jnp.int32, cs.shape, 1)

    ckr = _dot((hn * gkv_ref[...]).astype(BF16), wdkv_ref[...])
    t = ckr[:, MLA_KV_LORA:] * cs
    kr = t + pltpu.roll(t, MLA_ROPE, axis=1)
    kr_ref[...] = jnp.where(lane < MLA_ROPE, kr, 0.0).astype(BF16)
    ckv = _rms(ckr[:, :MLA_KV_LORA], gc_ref[...]).astype(BF16)
    kn_ref[...] = _dot(ckv, wuk_ref[...]).astype(BF16)
    v_ref[...] = _dot(ckv, wuv_ref[...]).astype(BF16)

    cq = _rms(_dot((hn * gq_ref[...]).astype(BF16), wdq_ref[...]), gcq_ref[...]).astype(BF16)
    qn_ref[...] = (_dot(cq, wuqn_ref[...]) * scale).astype(BF16)
    qr = _dot(cq, wuqr_ref[...])
    for hd in range(MLA_HEADS):
        sl = slice(hd * LANES, (hd + 1) * LANES)
        t = qr[:, sl] * cs
        qr_ref[:, sl] = ((t + pltpu.roll(t, MLA_ROPE, axis=1)) * scale).astype(BF16)


def _mla_proj(h, cs, gkv, wdkv, gc, wuk, wuv, gq, wdq, gcq, wuqn, wuqr, seq):
    n, d = h.shape
    tm = _tiles()["mla_proj"]
    hw = MLA_HEADS * LANES
    tok = lambda i: (i, 0)
    pos = lambda i: (i % (seq // tm), 0)
    scale = float((MLA_NOPE + MLA_ROPE) ** -0.5)
    wide = jax.ShapeDtypeStruct((n, hw), BF16)
    return pl.pallas_call(
        functools.partial(_mla_proj_kernel, scale=scale),
        grid=(n // tm,),
        in_specs=[
            pl.BlockSpec((tm, d), tok),
            pl.BlockSpec((tm, LANES), pos),
            _resident((1, d)), _resident(wdkv.shape), _resident((1, MLA_KV_LORA)),
            _resident(wuk.shape), _resident(wuv.shape),
            _resident((1, d)), _resident(wdq.shape), _resident((1, MLA_Q_LORA)),
            _resident(wuqn.shape), _resident(wuqr.shape),
        ],
        out_specs=[
            pl.BlockSpec((tm, hw), tok), pl.BlockSpec((tm, hw), tok), pl.BlockSpec((tm, hw), tok),
            pl.BlockSpec((tm, LANES), tok), pl.BlockSpec((tm, hw), tok),
        ],
        out_shape=[wide, wide, wide, jax.ShapeDtypeStruct((n, LANES), BF16), wide],
        compiler_params=_params(("arbitrary",)),
        name="mla_proj",
    )(h, cs, gkv, wdkv, gc, wuk, wuv, gq, wdq, gcq, wuqn, wuqr)


def _mla_attn_kernel(qn_ref, qr_ref, kn_ref, kr_ref, v_ref, o_ref, m_ref, l_ref, acc_ref):
    tq = qn_ref.shape[0]
    tk = _tiles()["attn_k"]
    qi = pl.program_id(2)
    q = jnp.concatenate([qn_ref[...], qr_ref[...]], axis=-1)

    m_ref[...] = jnp.full_like(m_ref, -jnp.inf)
    l_ref[...] = jnp.zeros_like(l_ref)
    acc_ref[...] = jnp.zeros_like(acc_ref)

    def step(j, masked):
        rows = pl.ds(pl.multiple_of(j * tk, tk), tk)
        k = jnp.concatenate([kn_ref[rows, :], kr_ref[rows, :]], axis=-1)
        s = _dot_nt(q, k)
        if masked:
            q_pos = qi * tq + lax.broadcasted_iota(jnp.int32, s.shape, 0)
            k_pos = j * tk + lax.broadcasted_iota(jnp.int32, s.shape, 1)
            s = jnp.where(k_pos <= q_pos, s, -jnp.inf)
        m_prev = m_ref[...]
        m_new = jnp.maximum(m_prev, jnp.max(s, axis=-1, keepdims=True))
        alpha = jnp.exp(m_prev - m_new)
        p = jnp.exp(s - m_new)
        l_ref[...] = alpha * l_ref[...] + jnp.sum(p, axis=-1, keepdims=True)
        acc_ref[...] = alpha * acc_ref[...] + _dot(p.astype(BF16), v_ref[rows, :])
        m_ref[...] = m_new

    n_full = (qi * tq) // tk
    lax.fori_loop(0, n_full, lambda j, c: (step(j, False), c)[1], 0)
    for jj in range(tq // tk):
        step(n_full + jj, True)

    o_ref[...] = (acc_ref[...] / l_ref[...]).astype(BF16)


def _mla_attn(qn, qr, kn, kr, v, batch, seq):
    tq = _tiles()["attn_q"]
    hw = MLA_HEADS * LANES
    qn, qr, kn, v = (a.reshape(batch, seq, hw) for a in (qn, qr, kn, v))
    kr = kr.reshape(batch, seq, LANES)
    q_blk = pl.BlockSpec((None, tq, LANES), lambda b, h, i: (b, i, h))
    kv_blk = pl.BlockSpec((None, seq, LANES), lambda b, h, i: (b, 0, h))
    kr_blk = pl.BlockSpec((None, seq, LANES), lambda b, h, i: (b, 0, 0))
    out = pl.pallas_call(
        _mla_attn_kernel,
        grid=(batch, MLA_HEADS, seq // tq),
        in_specs=[q_blk, q_blk, kv_blk, kr_blk, kv_blk],
        out_specs=q_blk,
        out_shape=jax.ShapeDtypeStruct((batch, seq, hw), BF16),
        scratch_shapes=[
            pltpu.VMEM((tq, 1), F32), pltpu.VMEM((tq, 1), F32), pltpu.VMEM((tq, MLA_V), F32),
        ],
        compiler_params=_params(("arbitrary", "arbitrary", "arbitrary")),
        name="mla_attn",
    )(qn, qr, kn, kr, v)
    return out.reshape(batch * seq, hw)


def _rope_partner(w):
    half = MLA_ROPE // 2
    return jnp.concatenate([-w[..., half:], w[..., :half]], axis=-1)


def _rope_table(seq):
    half = MLA_ROPE // 2
    inv_freq = ROPE_THETA ** (-jnp.arange(half, dtype=F32) / half)
    ang = jnp.arange(seq, dtype=F32)[:, None] * inv_freq[None, :]
    cos, sin = jnp.cos(ang), jnp.sin(ang)
    return jnp.concatenate([cos, cos, sin, sin], axis=-1)


def kernel(x, hgrn_norm, hgrn_w_q, hgrn_w_f, hgrn_w_i, hgrn_w_g, hgrn_g_norm, hgrn_w_o, hgrn_lb_logits, mla_norm, mla_w_dq, mla_q_norm, mla_w_uq, mla_w_o, kv_in_norm, kv_w_dkv, kv_norm, kv_w_uk, kv_w_uv, mlp_norm, mlp_w_up, mlp_w_down, final_norm):
    batch, seq, d = x.shape
    assert d == D_MODEL and hgrn_w_q.shape[0] == 1 and mla_w_dq.shape[0] == 1
    assert seq % max(_tiles().values()) == 0
    n = batch * seq
    x2 = x.reshape(n, d)
    row = lambda g: g.reshape(1, -1).astype(F32)

    w4 = jnp.stack([hgrn_w_q[0], hgrn_w_f[0], hgrn_w_i[0], hgrn_w_g[0]]).astype(BF16)
    q, f, v, gate = _hgrn_proj(x2, row(hgrn_norm[0]), hgrn_lb_logits.astype(F32), w4)
    og = _hgrn_scan(q, f, v, gate, row(hgrn_g_norm[0]), batch, seq)
    h = _resid_mlp(x2, og, hgrn_w_o[0].astype(BF16), row(mlp_norm[0]),
                   mlp_w_up[0].astype(BF16), mlp_w_down[0].astype(BF16), row(final_norm), False)

    w_rope = kv_w_dkv[:, MLA_KV_LORA:]
    wdkv = jnp.concatenate([kv_w_dkv, _rope_partner(w_rope)], axis=-1).astype(BF16)
    wuq = mla_w_uq[0].reshape(MLA_Q_LORA, MLA_HEADS, MLA_NOPE + MLA_ROPE)
    wuqn = wuq[..., :MLA_NOPE].reshape(MLA_Q_LORA, MLA_HEADS * MLA_NOPE).astype(BF16)
    wq_rope = wuq[..., MLA_NOPE:]
    wuqr = jnp.concatenate([wq_rope, _rope_partner(wq_rope)], axis=-1)
    wuqr = wuqr.reshape(MLA_Q_LORA, MLA_HEADS * LANES).astype(BF16)
    qn, qr, kn, kr, vv = _mla_proj(
        h, _rope_table(seq), row(kv_in_norm), wdkv, row(kv_norm),
        kv_w_uk.astype(BF16), kv_w_uv.astype(BF16),
        row(mla_norm[0]), mla_w_dq[0].astype(BF16), row(mla_q_norm[0]), wuqn, wuqr, seq)

    attn = _mla_attn(qn, qr, kn, kr, vv, batch, seq)
    out = _resid_mlp(h, attn, mla_w_o[0].astype(BF16), row(mlp_norm[1]),
                     mlp_w_up[1].astype(BF16), mlp_w_down[1].astype(BF16), row(final_norm), True)
    return out.reshape(batch, seq, d)
```

```python
import functools
import math

import jax
import jax.numpy as jnp
from jax import lax
from jax.experimental import pallas as pl
from jax.experimental.pallas import tpu as pltpu

F32 = jnp.float32
BF16 = jnp.bfloat16

D_MODEL = 1024
HGRN_HEADS = 8
HGRN_DK = 128
MLA_HEADS = 16
MLA_NOPE = 128
MLA_ROPE = 64
MLA_V = 128
MLA_Q_LORA = 256
MLA_KV_LORA = 256
ROPE_THETA = 10000.0
D_FF = 4 * D_MODEL
EPS = 1e-6

LANES = 128
VMEM_LIMIT_BYTES = 56 * 1024 * 1024

SCAN_CHUNK = 256
ATTN_HEADS_PER_STEP = 4
SUM_ROWS = 16


def _tiles():
    return dict(proj=512, scan=512, mlp=512, mla_proj=512, attn_q=512, attn_k=512, ff=1024)


def _rms(x, gain):
    return x * lax.rsqrt(jnp.mean(x * x, axis=-1, keepdims=True) + EPS) * gain


def _sigmoid(x):
    return 1.0 / (1.0 + jnp.exp(-x))


def _dot(a, b):
    return jnp.dot(a, b, preferred_element_type=F32)


def _dot_nt(a, b):
    return lax.dot_general(a, b, (((1,), (1,)), ((), ())), preferred_element_type=F32)


def _dot_tn(a, b):
    return lax.dot_general(a, b, (((0,), (0,)), ((), ())), preferred_element_type=F32)


def _resident(shape):
    nd = len(shape)
    return pl.BlockSpec(shape, lambda *_: (0,) * nd, pipeline_mode=pl.Buffered(1))


def _params(semantics):
    return pltpu.CompilerParams(dimension_semantics=semantics, vmem_limit_bytes=VMEM_LIMIT_BYTES)


def _hgrn_proj_kernel(x_ref, gain_ref, lb_ref, w_ref, q_ref, f_ref, v_ref, g_ref):
    xb = _rms(x_ref[...], gain_ref[...]).astype(BF16)
    yq = _dot(xb, w_ref[0])
    q_ref[...] = (yq * _sigmoid(yq)).astype(BF16)
    logits = lb_ref[...]
    e = jnp.exp(logits - jnp.max(logits, axis=0, keepdims=True))
    lb = e[0:1, :] / jnp.sum(e, axis=0, keepdims=True)
    f_ref[...] = lb + (1.0 - lb) * _sigmoid(_dot(xb, w_ref[1]))
    v_ref[...] = _dot(xb, w_ref[2]).astype(BF16)
    yg = _dot(xb, w_ref[3])
    g_ref[...] = (yg * _sigmoid(yg)).astype(BF16)


def _hgrn_proj(x2, gain, lb_logits, w4):
    n, d = x2.shape
    tm = _tiles()["proj"]
    tok = lambda i: (i, 0)
    return pl.pallas_call(
        _hgrn_proj_kernel,
        grid=(n // tm,),
        in_specs=[
            pl.BlockSpec((tm, d), tok),
            _resident((1, d)),
            _resident(lb_logits.shape),
            _resident(w4.shape),
        ],
        out_specs=[pl.BlockSpec((tm, d), tok)] * 4,
        out_shape=[
            jax.ShapeDtypeStruct((n, d), BF16),
            jax.ShapeDtypeStruct((n, d), F32),
            jax.ShapeDtypeStruct((n, d), BF16),
            jax.ShapeDtypeStruct((n, d), BF16),
        ],
        compiler_params=_params(("arbitrary",)),
        name="hgrn_proj",
    )(x2, gain, lb_logits, w4)


def _hgrn_scan_kernel(q_ref, f_ref, v_ref, g_ref, gn_ref, o_ref, st_ref):
    c = SCAN_CHUNK
    n_chunks = q_ref.shape[0] // c

    @pl.when(pl.program_id(2) == 0)
    def _():
        st_ref[...] = jnp.zeros_like(st_ref)

    row = lax.broadcasted_iota(jnp.int32, (c, LANES), 0)
    t_i = lax.broadcasted_iota(jnp.int32, (c, c), 0)
    s_i = lax.broadcasted_iota(jnp.int32, (c, c), 1)
    diff = t_i ^ s_i
    level = jnp.zeros((c, c), jnp.int32)
    h = 1
    while h < c:
        level = level + (diff >= h).astype(jnp.int32)
        h *= 2
    level = jnp.where(s_i <= t_i, level, -1)
    gn = gn_ref[...]

    def chunk(i, carry):
        r0 = pl.multiple_of(i * c, c)
        rows = pl.ds(r0, c)
        f = f_ref[rows, :]
        k = 1.0 - f
        q = q_ref[rows, :].astype(F32)
        vb = v_ref[rows, :]

        scores = jnp.where(level == 0, _dot_nt(q.astype(BF16), k.astype(BF16)), 0.0)
        g_pre = f
        h_suf = jnp.ones_like(f)
        tot = f
        h = 1
        lvl = 1
        while h < c:
            p = _dot_nt((q * g_pre).astype(BF16), (k * h_suf).astype(BF16))
            scores = jnp.where(level == lvl, p, scores)
            up = pltpu.roll(tot, h, axis=0)
            dn = pltpu.roll(tot, c - h, axis=0)
            upper = (row & h) != 0
            g_pre = g_pre * jnp.where(upper, up, 1.0)
            h_suf = h_suf * jnp.where(upper, 1.0, dn)
            tot = tot * jnp.where(upper, up, dn)
            h *= 2
            lvl += 1

        st = st_ref[...]
        o = _dot_nt((q * g_pre).astype(BF16), st.astype(BF16)) + _dot(scores.astype(BF16), vb)
        st_ref[...] = st * tot[0:1, :] + _dot_tn(vb, (k * h_suf).astype(BF16))

        on = _rms(o, gn)
        o_ref[rows, :] = (on * g_ref[rows, :].astype(F32)).astype(BF16)
        return carry

    lax.fori_loop(0, n_chunks, chunk, 0, unroll=True)


def _hgrn_scan(q, f, v, gate, g_norm, batch, seq):
    d = q.shape[-1]
    ts = _tiles()["scan"]
    q, f, v, gate = (a.reshape(batch, seq, d) for a in (q, f, v, gate))
    blk = pl.BlockSpec((None, ts, HGRN_DK), lambda b, h, s: (b, s, h))
    out = pl.pallas_call(
        _hgrn_scan_kernel,
        grid=(batch, HGRN_HEADS, seq // ts),
        in_specs=[blk, blk, blk, blk, _resident((1, HGRN_DK))],
        out_specs=blk,
        out_shape=jax.ShapeDtypeStruct((batch, seq, d), BF16),
        scratch_shapes=[pltpu.VMEM((HGRN_DK, HGRN_DK), F32)],
        compiler_params=_params(("arbitrary", "arbitrary", "arbitrary")),
        name="hgrn_scan",
    )(q, f, v, gate, g_norm)
    return out.reshape(batch * seq, d)


def _resid_mlp_kernel(h_ref, a_ref, wo_ref, gain_ref, wup_ref, wdn_ref, fin_ref, o_ref, *, final_norm):
    tf = _tiles()["ff"]
    h1 = h_ref[...] + _dot(a_ref[...], wo_ref[...])
    xb = _rms(h1, gain_ref[...]).astype(BF16)
    acc = h1
    for c0 in range(0, D_FF, tf):
        up = jnp.maximum(_dot(xb, wup_ref[:, c0:c0 + tf]), 0.0)
        acc = acc + _dot((up * up).astype(BF16), wdn_ref[c0:c0 + tf, :])
    if final_norm:
        acc = _rms(acc, fin_ref[...])
    o_ref[...] = acc


def _resid_mlp(h, a, wo, gain, wup, wdn, fin, final_norm):
    n, d = h.shape
    ka = a.shape[1]
    tm = _tiles()["mlp"]
    tok = lambda i: (i, 0)
    return pl.pallas_call(
        functools.partial(_resid_mlp_kernel, final_norm=final_norm),
        grid=(n // tm,),
        in_specs=[
            pl.BlockSpec((tm, d), tok),
            pl.BlockSpec((tm, ka), tok),
            _resident(wo.shape),
            _resident((1, d)),
            _resident(wup.shape),
            _resident(wdn.shape),
            _resident((1, d)),
        ],
        out_specs=pl.BlockSpec((tm, d), tok),
        out_shape=jax.ShapeDtypeStruct((n, d), F32),
        compiler_params=_params(("arbitrary",)),
        name="resid_mlp_final" if final_norm else "resid_mlp",
    )(h, a, wo, gain, wup, wdn, fin)


def _mla_proj_kernel(h_ref, cs_ref, gkv_ref, wdkv_ref, gc_ref, wuk_ref, wuv_ref,
                     gq_ref, wdq_ref, gcq_ref, wuqn_ref, wuqr_ref,
                     qn_ref, qr_ref, kn_ref, kr_ref, vt_ref, *, scale):
    tf = _tiles()["ff"]
    h = h_ref[...]
    hn = h * lax.rsqrt(jnp.mean(h * h, axis=-1, keepdims=True) + EPS)
    cs = cs_ref[...]
    lane = lax.broadcasted_iota(jnp.int32, cs.shape, 1)

    ckr = _dot((hn * gkv_ref[...]).astype(BF16), wdkv_ref[...])
    t = ckr[:, MLA_KV_LORA:] * cs
    kr = t + pltpu.roll(t, MLA_ROPE, axis=1)
    kr_ref[...] = jnp.where(lane < MLA_ROPE, kr, 0.0).astype(BF16)
    ckv = _rms(ckr[:, :MLA_KV_LORA], gc_ref[...]).astype(BF16)
    kn_ref[...] = _dot(ckv, wuk_ref[...]).astype(BF16)
    for r0 in range(0, vt_ref.shape[0], tf):
        vt_ref[r0:r0 + tf, :] = _dot_nt(wuv_ref[r0:r0 + tf, :], ckv).astype(BF16)

    cq = _rms(_dot((hn * gq_ref[...]).astype(BF16), wdq_ref[...]), gcq_ref[...]).astype(BF16)
    qn_ref[...] = (_dot(cq, wuqn_ref[...]) * scale).astype(BF16)
    qr = _dot(cq, wuqr_ref[...])
    for hd in range(MLA_HEADS):
        sl = slice(hd * LANES, (hd + 1) * LANES)
        t = qr[:, sl] * cs
        qr_ref[:, sl] = ((t + pltpu.roll(t, MLA_ROPE, axis=1)) * scale).astype(BF16)


def _mla_proj(h, cs, gkv, wdkv, gc, wuk, wuv, gq, wdq, gcq, wuqn, wuqr, seq):
    n, d = h.shape
    tm = _tiles()["mla_proj"]
    hw = MLA_HEADS * LANES
    tok = lambda i: (i, 0)
    pos = lambda i: (i % (seq // tm), 0)
    scale = float((MLA_NOPE + MLA_ROPE) ** -0.5 * math.log2(math.e))
    wide = jax.ShapeDtypeStruct((n, hw), BF16)
    return pl.pallas_call(
        functools.partial(_mla_proj_kernel, scale=scale),
        grid=(n // tm,),
        in_specs=[
            pl.BlockSpec((tm, d), tok),
            pl.BlockSpec((tm, LANES), pos),
            _resident((1, d)), _resident(wdkv.shape), _resident((1, MLA_KV_LORA)),
            _resident(wuk.shape), _resident(wuv.shape),
            _resident((1, d)), _resident(wdq.shape), _resident((1, MLA_Q_LORA)),
            _resident(wuqn.shape), _resident(wuqr.shape),
        ],
        out_specs=[
            pl.BlockSpec((tm, hw), tok), pl.BlockSpec((tm, hw), tok), pl.BlockSpec((tm, hw), tok),
            pl.BlockSpec((tm, LANES), tok),
            pl.BlockSpec((None, None, hw, tm), lambda i: (i // (seq // tm), i % (seq // tm), 0, 0)),
        ],
        out_shape=[wide, wide, wide, jax.ShapeDtypeStruct((n, LANES), BF16),
                   jax.ShapeDtypeStruct((n // seq, seq // tm, hw, tm), BF16)],
        compiler_params=_params(("arbitrary",)),
        name="mla_proj",
    )(h, cs, gkv, wdkv, gc, wuk, wuv, gq, wdq, gcq, wuqn, wuqr)


def _mla_attn_kernel(qn_ref, qr_ref, kn_ref, kr_ref, vt_ref, o_ref, m_ref, acc_ref):
    tq = qn_ref.shape[0]
    tk = vt_ref.shape[-1]
    heads = qn_ref.shape[1] // LANES
    qi = pl.program_id(2)

    m_ref[...] = jnp.full_like(m_ref, -jnp.inf)
    acc_ref[...] = jnp.zeros_like(acc_ref)
    ones = jnp.ones((SUM_ROWS, tk), BF16)

    def step(j, masked):
        rows = pl.ds(pl.multiple_of(j * tk, tk), tk)
        kr = kr_ref[rows, :]

        def scores(g):
            sl = slice(g * LANES, (g + 1) * LANES)
            k = jnp.concatenate([kn_ref[rows, sl], kr], axis=-1)
            q = jnp.concatenate([qn_ref[:, sl], qr_ref[:, sl]], axis=-1)
            s = _dot_nt(k, q)
            if masked:
                k_pos = j * tk + lax.broadcasted_iota(jnp.int32, s.shape, 0)
                q_pos = qi * tq + lax.broadcasted_iota(jnp.int32, s.shape, 1)
                s = jnp.where(k_pos <= q_pos, s, -jnp.inf)
            return s

        def consume(g, s):
            sl = slice(g * LANES, (g + 1) * LANES)
            m_prev = m_ref[g]
            m_new = jnp.maximum(m_prev, jnp.max(s, axis=0, keepdims=True))
            alpha = jnp.exp2(m_prev - m_new)
            p = jnp.exp2(s - m_new).astype(BF16)
            vt1 = jnp.concatenate([vt_ref[j, sl, :], ones], axis=0)
            acc_ref[g] = alpha * acc_ref[g] + _dot(vt1, p)
            m_ref[g] = m_new

        s_cur = scores(0)
        for g in range(heads):
            s_next = scores(g + 1) if g + 1 < heads else None
            consume(g, s_cur)
            s_cur = s_next

    n_full = (qi * tq) // tk
    lax.fori_loop(0, n_full, lambda j, c: (step(j, False), c)[1], 0)
    for jj in range(tq // tk):
        step(n_full + jj, True)

    for g in range(heads):
        acc = acc_ref[g]
        o = acc[:MLA_V, :] / acc[MLA_V:MLA_V + 1, :]
        o_ref[:, g * LANES:(g + 1) * LANES] = o.T.astype(BF16)


def _mla_attn(qn, qr, kn, kr, vt, batch, seq):
    tq = _tiles()["attn_q"]
    tk = vt.shape[-1]
    g = ATTN_HEADS_PER_STEP
    hw = MLA_HEADS * LANES
    qn, qr, kn = (a.reshape(batch, seq, hw) for a in (qn, qr, kn))
    kr = kr.reshape(batch, seq, LANES)
    q_blk = pl.BlockSpec((None, tq, g * LANES), lambda b, h, i: (b, i, h))
    kn_blk = pl.BlockSpec((None, seq, g * LANES), lambda b, h, i: (b, 0, h))
    kr_blk = pl.BlockSpec((None, seq, LANES), lambda b, h, i: (b, 0, 0))
    vt_blk = pl.BlockSpec((None, seq // tk, g * LANES, tk), lambda b, h, i: (b, 0, h, 0))
    out = pl.pallas_call(
        _mla_attn_kernel,
        grid=(batch, MLA_HEADS // g, seq // tq),
        in_specs=[q_blk, q_blk, kn_blk, kr_blk, vt_blk],
        out_specs=q_blk,
        out_shape=jax.ShapeDtypeStruct((batch, seq, hw), BF16),
        scratch_shapes=[
            pltpu.VMEM((g, 1, tq), F32), pltpu.VMEM((g, MLA_V + SUM_ROWS, tq), F32),
        ],
        compiler_params=_params(("arbitrary", "arbitrary", "arbitrary")),
        name="mla_attn",
    )(qn, qr, kn, kr, vt)
    return out.reshape(batch * seq, hw)


def _rope_partner(w):
    half = MLA_ROPE // 2
    return jnp.concatenate([-w[..., half:], w[..., :half]], axis=-1)


def _rope_table(seq):
    half = MLA_ROPE // 2
    inv_freq = ROPE_THETA ** (-jnp.arange(half, dtype=F32) / half)
    ang = jnp.arange(seq, dtype=F32)[:, None] * inv_freq[None, :]
    cos, sin = jnp.cos(ang), jnp.sin(ang)
    return jnp.concatenate([cos, cos, sin, sin], axis=-1)


def kernel(x, hgrn_norm, hgrn_w_q, hgrn_w_f, hgrn_w_i, hgrn_w_g, hgrn_g_norm, hgrn_w_o, hgrn_lb_logits, mla_norm, mla_w_dq, mla_q_norm, mla_w_uq, mla_w_o, kv_in_norm, kv_w_dkv, kv_norm, kv_w_uk, kv_w_uv, mlp_norm, mlp_w_up, mlp_w_down, final_norm):
    batch, seq, d = x.shape
    assert d == D_MODEL and hgrn_w_q.shape[0] == 1 and mla_w_dq.shape[0] == 1
    assert seq % max(_tiles().values()) == 0
    n = batch * seq
    x2 = x.reshape(n, d)
    row = lambda g: g.reshape(1, -1).astype(F32)

    w4 = jnp.stack([hgrn_w_q[0], hgrn_w_f[0], hgrn_w_i[0], hgrn_w_g[0]]).astype(BF16)
    q, f, v, gate = _hgrn_proj(x2, row(hgrn_norm[0]), hgrn_lb_logits.astype(F32), w4)
    og = _hgrn_scan(q, f, v, gate, row(hgrn_g_norm[0]), batch, seq)
    h = _resid_mlp(x2, og, hgrn_w_o[0].astype(BF16), row(mlp_norm[0]),
                   mlp_w_up[0].astype(BF16), mlp_w_down[0].astype(BF16), row(final_norm), False)

    w_rope = kv_w_dkv[:, MLA_KV_LORA:]
    wdkv = jnp.concatenate([kv_w_dkv, _rope_partner(w_rope)], axis=-1).astype(BF16)
    wuq = mla_w_uq[0].reshape(MLA_Q_LORA, MLA_HEADS, MLA_NOPE + MLA_ROPE)
    wuqn = wuq[..., :MLA_NOPE].reshape(MLA_Q_LORA, MLA_HEADS * MLA_NOPE).astype(BF16)
    wq_rope = wuq[..., MLA_NOPE:]
    wuqr = jnp.concatenate([wq_rope, _rope_partner(wq_rope)], axis=-1)
    wuqr = wuqr.reshape(MLA_Q_LORA, MLA_HEADS * LANES).astype(BF16)
    qn, qr, kn, kr, vv = _mla_proj(
        h, _rope_table(seq), row(kv_in_norm), wdkv, row(kv_norm),
        kv_w_uk.astype(BF16), kv_w_uv.T.astype(BF16),
        row(mla_norm[0]), mla_w_dq[0].astype(BF16), row(mla_q_norm[0]), wuqn, wuqr, seq)

    attn = _mla_attn(qn, qr, kn, kr, vv, batch, seq)
    out = _resid_mlp(h, attn, mla_w_o[0].astype(BF16), row(mlp_norm[1]),
                     mlp_w_up[1].astype(BF16), mlp_w_down[1].astype(BF16), row(final_norm), True)
    return out.reshape(batch, seq, d)
```

```python
import functools
import math

import jax
import jax.numpy as jnp
from jax import lax
from jax.experimental import pallas as pl
from jax.experimental.pallas import tpu as pltpu

F32 = jnp.float32
BF16 = jnp.bfloat16

D_MODEL = 1024
HGRN_HEADS = 8
HGRN_DK = 128
MLA_HEADS = 16
MLA_NOPE = 128
MLA_ROPE = 64
MLA_V = 128
MLA_Q_LORA = 256
MLA_KV_LORA = 256
ROPE_THETA = 10000.0
D_FF = 4 * D_MODEL
EPS = 1e-6

LANES = 128
VMEM_LIMIT_BYTES = 56 * 1024 * 1024

SCAN_CHUNK = 256
ATTN_HEADS_PER_STEP = 4
ATTN_Q_STRIP = 256
ATTN_LOOKAHEAD = 4
ATTN_PV_DELAY = 2
SUM_ROWS = 16


def _tiles():
    return dict(proj=512, scan=512, mlp=512, mla_proj=512, attn_q=512, attn_k=512, ff=1024)


def _rms(x, gain):
    return x * lax.rsqrt(jnp.mean(x * x, axis=-1, keepdims=True) + EPS) * gain


def _sigmoid(x):
    return 1.0 / (1.0 + jnp.exp(-x))


def _dot(a, b):
    return jnp.dot(a, b, preferred_element_type=F32)


def _dot_nt(a, b):
    return lax.dot_general(a, b, (((1,), (1,)), ((), ())), preferred_element_type=F32)


def _dot_tn(a, b):
    return lax.dot_general(a, b, (((0,), (0,)), ((), ())), preferred_element_type=F32)


def _resident(shape):
    nd = len(shape)
    return pl.BlockSpec(shape, lambda *_: (0,) * nd, pipeline_mode=pl.Buffered(1))


def _params(semantics):
    return pltpu.CompilerParams(dimension_semantics=semantics, vmem_limit_bytes=VMEM_LIMIT_BYTES)


def _hgrn_proj_kernel(x_ref, gain_ref, lb_ref, w_ref, q_ref, f_ref, v_ref, g_ref):
    xb = _rms(x_ref[...], gain_ref[...]).astype(BF16)
    yq = _dot(xb, w_ref[0])
    q_ref[...] = (yq * _sigmoid(yq)).astype(BF16)
    logits = lb_ref[...]
    e = jnp.exp(logits - jnp.max(logits, axis=0, keepdims=True))
    lb = e[0:1, :] / jnp.sum(e, axis=0, keepdims=True)
    f_ref[...] = lb + (1.0 - lb) * _sigmoid(_dot(xb, w_ref[1]))
    v_ref[...] = _dot(xb, w_ref[2]).astype(BF16)
    yg = _dot(xb, w_ref[3])
    g_ref[...] = (yg * _sigmoid(yg)).astype(BF16)


def _hgrn_proj(x2, gain, lb_logits, w4):
    n, d = x2.shape
    tm = _tiles()["proj"]
    tok = lambda i: (i, 0)
    return pl.pallas_call(
        _hgrn_proj_kernel,
        grid=(n // tm,),
        in_specs=[
            pl.BlockSpec((tm, d), tok),
            _resident((1, d)),
            _resident(lb_logits.shape),
            _resident(w4.shape),
        ],
        out_specs=[pl.BlockSpec((tm, d), tok)] * 4,
        out_shape=[
            jax.ShapeDtypeStruct((n, d), BF16),
            jax.ShapeDtypeStruct((n, d), F32),
            jax.ShapeDtypeStruct((n, d), BF16),
            jax.ShapeDtypeStruct((n, d), BF16),
        ],
        compiler_params=_params(("arbitrary",)),
        name="hgrn_proj",
    )(x2, gain, lb_logits, w4)


SUBLANES = 8


def _scan_pack_lhs(x):
    half = x.shape[0] // 2
    return jnp.concatenate([x[:half], x[half:]], axis=1)


def _scan_pack_rhs(x):
    half = x.shape[0] // 2
    z = jnp.zeros((half, x.shape[1]), x.dtype)
    return jnp.concatenate([jnp.concatenate([x[:half], z], axis=1),
                            jnp.concatenate([z, x[half:]], axis=1)], axis=0)


def _scan_pair_dot(lhs, rhs):
    return _dot_nt(_scan_pack_lhs(lhs), _scan_pack_rhs(rhs))


def _hgrn_scan_kernel(q_ref, f_ref, v_ref, g_ref, gn_ref, o_ref, st_ref):
    c = SCAN_CHUNK
    half = c // 2
    nv = c // SUBLANES
    n_chunks = q_ref.shape[0] // c

    @pl.when(pl.program_id(2) == 0)
    def _():
        st_ref[...] = jnp.zeros_like(st_ref)

    t_i = lax.broadcasted_iota(jnp.int32, (half, c), 0)
    s_i = lax.broadcasted_iota(jnp.int32, (half, c), 1) & (half - 1)
    diff = t_i ^ s_i
    level = jnp.zeros((half, c), jnp.int32)
    h = 1
    while h < half:
        level = level + (diff >= h).astype(jnp.int32)
        h *= 2
    level = jnp.where(s_i <= t_i, level, -1)
    sub = lax.broadcasted_iota(jnp.int32, (SUBLANES, LANES), 0)
    zero = jnp.zeros((SUBLANES, LANES), F32)
    gn = gn_ref[...]

    def tiles(x):
        return [x[SUBLANES * a:SUBLANES * (a + 1)] for a in range(nv)]

    def cat(pieces):
        return jnp.concatenate(pieces, axis=0).astype(BF16)

    def chunk(i, carry):
        rows = pl.ds(pl.multiple_of(i * c, c), c)
        f = f_ref[rows, :]
        qb = q_ref[rows, :]
        vb = v_ref[rows, :]
        fv = tiles(f)
        kv = [1.0 - x for x in fv]
        qv = tiles(qb.astype(F32))

        sc = jnp.where(level == 0, _scan_pair_dot(qb, cat(kv)), 0.0)
        lvl = 1
        g_pre = list(fv)
        h_suf = [None] * nv
        tot = list(fv)

        h = 1
        while h < SUBLANES:
            qt = [qv[a] * g_pre[a] for a in range(nv)]
            kt = [kv[a] if h_suf[a] is None else kv[a] * h_suf[a] for a in range(nv)]
            sc = jnp.where(level == lvl, _scan_pair_dot(cat(qt), cat(kt)), sc)
            lvl += 1
            upper = (sub & h) != 0
            for a in range(nv):
                up = pltpu.roll(tot[a], h, axis=0)
                dn = pltpu.roll(tot[a], SUBLANES - h, axis=0)
                g_pre[a] = g_pre[a] * jnp.where(upper, up, 1.0)
                hs = jnp.where(upper, 1.0, dn)
                h_suf[a] = hs if h_suf[a] is None else h_suf[a] * hs
                tot[a] = tot[a] * jnp.where(upper, up, dn)
            h *= 2

        p_top = None
        m = 1
        while m < nv:
            is_upper = [(a // m) % 2 == 1 for a in range(nv)]
            qt = [qv[a] * g_pre[a] if is_upper[a] else zero for a in range(nv)]
            kt = [zero if is_upper[a] else kv[a] * h_suf[a] for a in range(nv)]
            if 2 * m < nv:
                sc = jnp.where(level == lvl, _scan_pair_dot(cat(qt), cat(kt)), sc)
                lvl += 1
            else:
                p_top = _dot_nt(cat(qt[nv // 2:]), cat(kt[:nv // 2]))
            for b in range(0, nv, 2 * m):
                t_lo, t_hi = tot[b], tot[b + m]
                for a in range(b + m, b + 2 * m):
                    g_pre[a] = g_pre[a] * t_lo
                for a in range(b, b + m):
                    h_suf[a] = h_suf[a] * t_hi
                t_all = t_lo * t_hi
                for a in range(b, b + 2 * m):
                    tot[a] = t_all
            m *= 2

        d = sc.astype(BF16)
        o_lo = _dot(d[:, :half], vb[:half])
        o_hi = _dot(jnp.concatenate([p_top.astype(BF16), d[:, half:]], axis=1), vb)

        st = st_ref[...]
        stb = st.astype(BF16)
        q_all = cat([qv[a] * g_pre[a] for a in range(nv)])
        o_st = _scan_pair_dot(q_all, jnp.concatenate([stb, stb], axis=0))
        o = jnp.concatenate([o_st[:, :LANES] + o_lo, o_st[:, LANES:] + o_hi], axis=0)
        k_all = cat([kv[a] * h_suf[a] for a in range(nv)])
        st_ref[...] = st * tot[0][0:1, :] + _dot_tn(vb, k_all)

        on = _rms(o, gn)
        o_ref[rows, :] = (on * g_ref[rows, :].astype(F32)).astype(BF16)
        return carry

    lax.fori_loop(0, n_chunks, chunk, 0, unroll=True)


def _hgrn_scan(q, f, v, gate, g_norm, batch, seq):
    d = q.shape[-1]
    ts = _tiles()["scan"]
    q, f, v, gate = (a.reshape(batch, seq, d) for a in (q, f, v, gate))
    blk = pl.BlockSpec((None, ts, HGRN_DK), lambda b, h, s: (b, s, h))
    out = pl.pallas_call(
        _hgrn_scan_kernel,
        grid=(batch, HGRN_HEADS, seq // ts),
        in_specs=[blk, blk, blk, blk, _resident((1, HGRN_DK))],
        out_specs=blk,
        out_shape=jax.ShapeDtypeStruct((batch, seq, d), BF16),
        scratch_shapes=[pltpu.VMEM((HGRN_DK, HGRN_DK), F32)],
        compiler_params=_params(("arbitrary", "arbitrary", "arbitrary")),
        name="hgrn_scan",
    )(q, f, v, gate, g_norm)
    return out.reshape(batch * seq, d)


def _resid_mlp_kernel(h_ref, a_ref, wo_ref, gain_ref, wup_ref, wdn_ref, fin_ref, o_ref, *, final_norm):
    tf = _tiles()["ff"]
    h1 = h_ref[...] + _dot(a_ref[...], wo_ref[...])
    xb = _rms(h1, gain_ref[...]).astype(BF16)
    acc = h1
    for c0 in range(0, D_FF, tf):
        up = jnp.maximum(_dot(xb, wup_ref[:, c0:c0 + tf]), 0.0)
        acc = acc + _dot((up * up).astype(BF16), wdn_ref[c0:c0 + tf, :])
    if final_norm:
        acc = _rms(acc, fin_ref[...])
    o_ref[...] = acc


def _resid_mlp(h, a, wo, gain, wup, wdn, fin, final_norm):
    n, d = h.shape
    ka = a.shape[1]
    tm = _tiles()["mlp"]
    tok = lambda i: (i, 0)
    return pl.pallas_call(
        functools.partial(_resid_mlp_kernel, final_norm=final_norm),
        grid=(n // tm,),
        in_specs=[
            pl.BlockSpec((tm, d), tok),
            pl.BlockSpec((tm, ka), tok),
            _resident(wo.shape),
            _resident((1, d)),
            _resident(wup.shape),
            _resident(wdn.shape),
            _resident((1, d)),
        ],
        out_specs=pl.BlockSpec((tm, d), tok),
        out_shape=jax.ShapeDtypeStruct((n, d), F32),
        compiler_params=_params(("arbitrary",)),
        name="resid_mlp_final" if final_norm else "resid_mlp",
    )(h, a, wo, gain, wup, wdn, fin)


def _mla_proj_kernel(h_ref, cs_ref, gkv_ref, wdkv_ref, gc_ref, wuk_ref, wuv_ref,
                     gq_ref, wdq_ref, gcq_ref, wuqn_ref, wuqr_ref,
                     qn_ref, qr_ref, kn_ref, kr_ref, vt_ref, *, scale):
    tf = _tiles()["ff"]
    h = h_ref[...]
    hn = h * lax.rsqrt(jnp.mean(h * h, axis=-1, keepdims=True) + EPS)
    cs = cs_ref[...]
    lane = lax.broadcasted_iota(jnp.int32, cs.shape, 1)

    ckr = _dot((hn * gkv_ref[...]).astype(BF16), wdkv_ref[...])
    t = ckr[:, MLA_KV_LORA:] * cs
    kr = t + pltpu.roll(t, MLA_ROPE, axis=1)
    kr_ref[...] = jnp.where(lane < MLA_ROPE, kr, 0.0).astype(BF16)
    ckv = _rms(ckr[:, :MLA_KV_LORA], gc_ref[...]).astype(BF16)
    kn_ref[...] = _dot(ckv, wuk_ref[...]).astype(BF16)
    tk = vt_ref.shape[-1]
    for kb in range(vt_ref.shape[0]):
        for r0 in range(0, vt_ref.shape[1], tf):
            vt_ref[kb, r0:r0 + tf, :] = _dot_nt(
                wuv_ref[r0:r0 + tf, :], ckv[kb * tk:(kb + 1) * tk, :]).astype(BF16)

    cq = _rms(_dot((hn * gq_ref[...]).astype(BF16), wdq_ref[...]), gcq_ref[...]).astype(BF16)
    qn_ref[...] = (_dot(cq, wuqn_ref[...]) * scale).astype(BF16)
    qr = _dot(cq, wuqr_ref[...])
    for hd in range(MLA_HEADS):
        sl = slice(hd * LANES, (hd + 1) * LANES)
        t = qr[:, sl] * cs
        qr_ref[:, sl] = ((t + pltpu.roll(t, MLA_ROPE, axis=1)) * scale).astype(BF16)


def _mla_proj(h, cs, gkv, wdkv, gc, wuk, wuv, gq, wdq, gcq, wuqn, wuqr, seq):
    n, d = h.shape
    tm = _tiles()["mla_proj"]
    tk = _tiles()["attn_k"]
    hw = MLA_HEADS * LANES
    tok = lambda i: (i, 0)
    pos = lambda i: (i % (seq // tm), 0)
    scale = float((MLA_NOPE + MLA_ROPE) ** -0.5 * math.log2(math.e))
    wide = jax.ShapeDtypeStruct((n, hw), BF16)
    return pl.pallas_call(
        functools.partial(_mla_proj_kernel, scale=scale),
        grid=(n // tm,),
        in_specs=[
            pl.BlockSpec((tm, d), tok),
            pl.BlockSpec((tm, LANES), pos),
            _resident((1, d)), _resident(wdkv.shape), _resident((1, MLA_KV_LORA)),
            _resident(wuk.shape), _resident(wuv.shape),
            _resident((1, d)), _resident(wdq.shape), _resident((1, MLA_Q_LORA)),
            _resident(wuqn.shape), _resident(wuqr.shape),
        ],
        out_specs=[
            pl.BlockSpec((tm, hw), tok), pl.BlockSpec((tm, hw), tok), pl.BlockSpec((tm, hw), tok),
            pl.BlockSpec((tm, LANES), tok),
            pl.BlockSpec((tm // tk, hw, tk), lambda i: (i, 0, 0)),
        ],
        out_shape=[wide, wide, wide, jax.ShapeDtypeStruct((n, LANES), BF16),
                   jax.ShapeDtypeStruct((n // tk, hw, tk), BF16)],
        compiler_params=_params(("arbitrary",)),
        name="mla_proj",
    )(h, cs, gkv, wdkv, gc, wuk, wuv, gq, wdq, gcq, wuqn, wuqr)


def _mla_attn_kernel(qn_ref, qr_ref, kn_ref, kr_ref, vt_ref, o_ref, m_ref, acc_ref):
    tq = qn_ref.shape[0]
    tk = vt_ref.shape[-1]
    assert tq == tk
    heads = qn_ref.shape[1] // LANES
    strips = tq // ATTN_Q_STRIP
    qi = pl.program_id(2)

    m_ref[...] = jnp.full_like(m_ref, -jnp.inf)
    acc_ref[...] = jnp.zeros_like(acc_ref)
    ones = jnp.ones((SUM_ROWS, tk), BF16)

    def step(j, diagonal):
        r0 = pl.multiple_of(j * tk, tk)

        def scores(c):
            g, st = divmod(c, strips)
            sl = slice(g * LANES, (g + 1) * LANES)
            cols = slice(st * ATTN_Q_STRIP, (st + 1) * ATTN_Q_STRIP)
            rows = pl.ds(r0, (st + 1) * ATTN_Q_STRIP if diagonal else tk)
            k = jnp.concatenate([kn_ref[rows, sl], kr_ref[rows, :]], axis=-1)
            q = jnp.concatenate([qn_ref[cols, sl], qr_ref[cols, sl]], axis=-1)
            s = _dot_nt(k, q)
            if diagonal:
                k_pos = lax.broadcasted_iota(jnp.int32, s.shape, 0)
                q_pos = st * ATTN_Q_STRIP + lax.broadcasted_iota(jnp.int32, s.shape, 1)
                s = jnp.where(k_pos <= q_pos, s, -jnp.inf)
            return s

        def softmax(c, s):
            g, st = divmod(c, strips)
            cols = slice(st * ATTN_Q_STRIP, (st + 1) * ATTN_Q_STRIP)
            m_prev = m_ref[g, :, cols]
            m_new = jnp.maximum(m_prev, jnp.max(s, axis=0, keepdims=True))
            m_ref[g, :, cols] = m_new
            return jnp.exp2(m_prev - m_new), jnp.exp2(s - m_new).astype(BF16)

        def accumulate(c, alpha, p):
            g, st = divmod(c, strips)
            sl = slice(g * LANES, (g + 1) * LANES)
            cols = slice(st * ATTN_Q_STRIP, (st + 1) * ATTN_Q_STRIP)
            nk = p.shape[0]
            vt1 = jnp.concatenate([vt_ref[j, sl, :nk], ones[:, :nk]], axis=0)
            acc_ref[g, :, cols] = alpha * acc_ref[g, :, cols] + _dot(vt1, p)

        n_chains = heads * strips
        s_q, p_q = {}, {}
        for t in range(n_chains + ATTN_LOOKAHEAD + ATTN_PV_DELAY):
            if t < n_chains:
                s_q[t] = scores(t)
            c = t - ATTN_LOOKAHEAD
            if 0 <= c < n_chains:
                p_q[c] = softmax(c, s_q.pop(c))
            c = t - ATTN_LOOKAHEAD - ATTN_PV_DELAY
            if 0 <= c < n_chains:
                accumulate(c, *p_q.pop(c))

    lax.fori_loop(0, qi, lambda j, c: (step(j, False), c)[1], 0)
    step(qi, True)

    for g in range(heads):
        acc = acc_ref[g]
        o = acc[:MLA_V, :] / acc[MLA_V:MLA_V + 1, :]
        o_ref[:, g * LANES:(g + 1) * LANES] = o.T.astype(BF16)


def _mla_attn(qn, qr, kn, kr, vt, batch, seq):
    tq = _tiles()["attn_q"]
    tk = vt.shape[-1]
    g = ATTN_HEADS_PER_STEP
    hw = MLA_HEADS * LANES
    qn, qr, kn = (a.reshape(batch, seq, hw) for a in (qn, qr, kn))
    vt = vt.reshape(batch, seq // tk, hw, tk)
    kr = kr.reshape(batch, seq, LANES)
    q_blk = pl.BlockSpec((None, tq, g * LANES), lambda b, h, i: (b, i, h))
    kn_blk = pl.BlockSpec((None, seq, g * LANES), lambda b, h, i: (b, 0, h))
    kr_blk = pl.BlockSpec((None, seq, LANES), lambda b, h, i: (b, 0, 0))
    vt_blk = pl.BlockSpec((None, seq // tk, g * LANES, tk), lambda b, h, i: (b, 0, h, 0))
    out = pl.pallas_call(
        _mla_attn_kernel,
        grid=(batch, MLA_HEADS // g, seq // tq),
        in_specs=[q_blk, q_blk, kn_blk, kr_blk, vt_blk],
        out_specs=q_blk,
        out_shape=jax.ShapeDtypeStruct((batch, seq, hw), BF16),
        scratch_shapes=[
            pltpu.VMEM((g, 1, tq), F32), pltpu.VMEM((g, MLA_V + SUM_ROWS, tq), F32),
        ],
        compiler_params=_params(("arbitrary", "arbitrary", "arbitrary")),
        name="mla_attn",
    )(qn, qr, kn, kr, vt)
    return out.reshape(batch * seq, hw)


def _rope_partner(w):
    half = MLA_ROPE // 2
    return jnp.concatenate([-w[..., half:], w[..., :half]], axis=-1)


def _rope_table(seq):
    half = MLA_ROPE // 2
    inv_freq = ROPE_THETA ** (-jnp.arange(half, dtype=F32) / half)
    ang = jnp.arange(seq, dtype=F32)[:, None] * inv_freq[None, :]
    cos, sin = jnp.cos(ang), jnp.sin(ang)
    return jnp.concatenate([cos, cos, sin, sin], axis=-1)


def kernel(x, hgrn_norm, hgrn_w_q, hgrn_w_f, hgrn_w_i, hgrn_w_g, hgrn_g_norm, hgrn_w_o, hgrn_lb_logits, mla_norm, mla_w_dq, mla_q_norm, mla_w_uq, mla_w_o, kv_in_norm, kv_w_dkv, kv_norm, kv_w_uk, kv_w_uv, mlp_norm, mlp_w_up, mlp_w_down, final_norm):
    batch, seq, d = x.shape
    assert d == D_MODEL and hgrn_w_q.shape[0] == 1 and mla_w_dq.shape[0] == 1
    assert seq % max(_tiles().values()) == 0
    n = batch * seq
    x2 = x.reshape(n, d)
    row = lambda g: g.reshape(1, -1).astype(F32)

    w4 = jnp.stack([hgrn_w_q[0], hgrn_w_f[0], hgrn_w_i[0], hgrn_w_g[0]]).astype(BF16)
    q, f, v, gate = _hgrn_proj(x2, row(hgrn_norm[0]), hgrn_lb_logits.astype(F32), w4)
    og = _hgrn_scan(q, f, v, gate, row(hgrn_g_norm[0]), batch, seq)
    h = _resid_mlp(x2, og, hgrn_w_o[0].astype(BF16), row(mlp_norm[0]),
                   mlp_w_up[0].astype(BF16), mlp_w_down[0].astype(BF16), row(final_norm), False)

    w_rope = kv_w_dkv[:, MLA_KV_LORA:]
    wdkv = jnp.concatenate([kv_w_dkv, _rope_partner(w_rope)], axis=-1).astype(BF16)
    wuq = mla_w_uq[0].reshape(MLA_Q_LORA, MLA_HEADS, MLA_NOPE + MLA_ROPE)
    wuqn = wuq[..., :MLA_NOPE].reshape(MLA_Q_LORA, MLA_HEADS * MLA_NOPE).astype(BF16)
    wq_rope = wuq[..., MLA_NOPE:]
    wuqr = jnp.concatenate([wq_rope, _rope_partner(wq_rope)], axis=-1)
    wuqr = wuqr.reshape(MLA_Q_LORA, MLA_HEADS * LANES).astype(BF16)
    qn, qr, kn, kr, vv = _mla_proj(
        h, _rope_table(seq), row(kv_in_norm), wdkv, row(kv_norm),
        kv_w_uk.astype(BF16), kv_w_uv.T.astype(BF16),
        row(mla_norm[0]), mla_w_dq[0].astype(BF16), row(mla_q_norm[0]), wuqn, wuqr, seq)

    attn = _mla_attn(qn, qr, kn, kr, vv, batch, seq)
    out = _resid_mlp(h, attn, mla_w_o[0].astype(BF16), row(mlp_norm[1]),
                     mlp_w_up[1].astype(BF16), mlp_w_down[1].astype(BF16), row(final_norm), True)
    return out.reshape(batch, seq, d)
```

```python
import functools
import math

import jax
import jax.numpy as jnp
from jax import lax
from jax.experimental import pallas as pl
from jax.experimental.pallas import tpu as pltpu

F32 = jnp.float32
BF16 = jnp.bfloat16

D_MODEL = 1024
HGRN_HEADS = 8
HGRN_DK = 128
MLA_HEADS = 16
MLA_NOPE = 128
MLA_ROPE = 64
MLA_V = 128
MLA_Q_LORA = 256
MLA_KV_LORA = 256
ROPE_THETA = 10000.0
D_FF = 4 * D_MODEL
EPS = 1e-6

LANES = 128
VMEM_LIMIT_BYTES = 56 * 1024 * 1024

SCAN_CHUNK = 256
ATTN_HEADS_PER_STEP = 8
ATTN_Q_STRIP = 256
ATTN_LOOKAHEAD = 4
ATTN_PV_DELAY = 2
SUM_ROWS = 16


def _tiles():
    return dict(proj=512, scan=1024, mlp=512, mla_proj=512, attn_q=512, attn_k=512, ff=1024)


def _rms(x, gain):
    return x * lax.rsqrt(jnp.mean(x * x, axis=-1, keepdims=True) + EPS) * gain


def _sigmoid(x):
    return 1.0 / (1.0 + jnp.exp(-x))


def _dot(a, b):
    return jnp.dot(a, b, preferred_element_type=F32)


def _dot_nt(a, b):
    return lax.dot_general(a, b, (((1,), (1,)), ((), ())), preferred_element_type=F32)


def _dot_tn(a, b):
    return lax.dot_general(a, b, (((0,), (0,)), ((), ())), preferred_element_type=F32)


def _resident(shape):
    nd = len(shape)
    return pl.BlockSpec(shape, lambda *_: (0,) * nd, pipeline_mode=pl.Buffered(1))


def _params(semantics):
    return pltpu.CompilerParams(dimension_semantics=semantics, vmem_limit_bytes=VMEM_LIMIT_BYTES)


def _hgrn_proj_kernel(x_ref, gain_ref, lb_ref, w_ref, q_ref, f_ref, v_ref, g_ref):
    xb = _rms(x_ref[...], gain_ref[...]).astype(BF16)
    yq = _dot(xb, w_ref[0])
    q_ref[...] = (yq * _sigmoid(yq)).astype(BF16)
    logits = lb_ref[...]
    e = jnp.exp(logits - jnp.max(logits, axis=0, keepdims=True))
    lb = e[0:1, :] / jnp.sum(e, axis=0, keepdims=True)
    f_ref[...] = lb + (1.0 - lb) * _sigmoid(_dot(xb, w_ref[1]))
    v_ref[...] = _dot(xb, w_ref[2]).astype(BF16)
    yg = _dot(xb, w_ref[3])
    g_ref[...] = (yg * _sigmoid(yg)).astype(BF16)


def _hgrn_proj(x2, gain, lb_logits, w4):
    n, d = x2.shape
    tm = _tiles()["proj"]
    tok = lambda i: (i, 0)
    return pl.pallas_call(
        _hgrn_proj_kernel,
        grid=(n // tm,),
        in_specs=[
            pl.BlockSpec((tm, d), tok),
            _resident((1, d)),
            _resident(lb_logits.shape),
            _resident(w4.shape),
        ],
        out_specs=[pl.BlockSpec((tm, d), tok)] * 4,
        out_shape=[
            jax.ShapeDtypeStruct((n, d), BF16),
            jax.ShapeDtypeStruct((n, d), F32),
            jax.ShapeDtypeStruct((n, d), BF16),
            jax.ShapeDtypeStruct((n, d), BF16),
        ],
        compiler_params=_params(("arbitrary",)),
        name="hgrn_proj",
    )(x2, gain, lb_logits, w4)


SUBLANES = 8


def _scan_pack_lhs(x):
    half = x.shape[0] // 2
    return jnp.concatenate([x[:half], x[half:]], axis=1)


def _scan_pack_rhs(x):
    half = x.shape[0] // 2
    z = jnp.zeros((half, x.shape[1]), x.dtype)
    return jnp.concatenate([jnp.concatenate([x[:half], z], axis=1),
                            jnp.concatenate([z, x[half:]], axis=1)], axis=0)


def _scan_pair_dot(lhs, rhs):
    return _dot_nt(_scan_pack_lhs(lhs), _scan_pack_rhs(rhs))


def _hgrn_scan_kernel(q_ref, f_ref, v_ref, g_ref, gn_ref, o_ref, st_ref):
    c = SCAN_CHUNK
    half = c // 2
    nv = c // SUBLANES
    n_chunks = q_ref.shape[0] // c

    @pl.when(pl.program_id(2) == 0)
    def _():
        st_ref[...] = jnp.zeros_like(st_ref)

    t_i = lax.broadcasted_iota(jnp.int32, (half, c), 0)
    s_i = lax.broadcasted_iota(jnp.int32, (half, c), 1) & (half - 1)
    diff = t_i ^ s_i
    level = jnp.zeros((half, c), jnp.int32)
    h = 1
    while h < half:
        level = level + (diff >= h).astype(jnp.int32)
        h *= 2
    level = jnp.where(s_i <= t_i, level, -1)
    sub = lax.broadcasted_iota(jnp.int32, (SUBLANES, LANES), 0)
    zero = jnp.zeros((SUBLANES, LANES), F32)
    gn = gn_ref[...]

    def tiles(x):
        return [x[SUBLANES * a:SUBLANES * (a + 1)] for a in range(nv)]

    def cat(pieces):
        return jnp.concatenate(pieces, axis=0).astype(BF16)

    def levels(i):
        rows = pl.ds(i * c, c)
        f = f_ref[rows, :]
        qb = q_ref[rows, :]
        vb = v_ref[rows, :]
        fv = tiles(f)
        kv = [1.0 - x for x in fv]
        qv = tiles(qb.astype(F32))

        sc = jnp.where(level == 0, _scan_pair_dot(qb, cat(kv)), 0.0)
        lvl = 1
        g_pre = list(fv)
        h_suf = [None] * nv
        tot = list(fv)

        h = 1
        while h < SUBLANES:
            qt = [qv[a] * g_pre[a] for a in range(nv)]
            kt = [kv[a] if h_suf[a] is None else kv[a] * h_suf[a] for a in range(nv)]
            sc = jnp.where(level == lvl, _scan_pair_dot(cat(qt), cat(kt)), sc)
            lvl += 1
            upper = (sub & h) != 0
            for a in range(nv):
                up = pltpu.roll(tot[a], h, axis=0)
                dn = up if 2 * h == SUBLANES else pltpu.roll(tot[a], SUBLANES - h, axis=0)
                g_pre[a] = g_pre[a] * jnp.where(upper, up, 1.0)
                hs = jnp.where(upper, 1.0, dn)
                h_suf[a] = hs if h_suf[a] is None else h_suf[a] * hs
                tot[a] = tot[a] * (up if 2 * h == SUBLANES else jnp.where(upper, up, dn))
            h *= 2

        p_top = None
        m = 1
        while m < nv:
            is_upper = [(a // m) % 2 == 1 for a in range(nv)]
            qt = [qv[a] * g_pre[a] if is_upper[a] else zero for a in range(nv)]
            kt = [zero if is_upper[a] else kv[a] * h_suf[a] for a in range(nv)]
            if 4 * m < nv:
                sc = jnp.where(level == lvl, _scan_pair_dot(cat(qt), cat(kt)), sc)
                lvl += 1
            elif 2 * m < nv:
                sc = sc + _scan_pair_dot(cat(qt), cat(kt))
                lvl += 1
            else:
                p_top = _dot_nt(cat(qt[nv // 2:]), cat(kt[:nv // 2]))
            for b in range(0, nv, 2 * m):
                t_lo, t_hi = tot[b], tot[b + m]
                for a in range(b + m, b + 2 * m):
                    g_pre[a] = g_pre[a] * t_lo
                for a in range(b, b + m):
                    h_suf[a] = h_suf[a] * t_hi
                t_all = t_lo * t_hi
                for a in range(b, b + 2 * m):
                    tot[a] = t_all
            m *= 2

        q_all = cat([qv[a] * g_pre[a] for a in range(nv)])
        k_all = cat([kv[a] * h_suf[a] for a in range(nv)])
        return rows, vb, sc, p_top, q_all, k_all, tot[0][0:1, :]

    def finish(rows, vb, sc, p_top, q_all, k_all, decay):
        d = sc.astype(BF16)
        o_lo = _dot(d[:, :half], vb[:half])
        o_hi = _dot(jnp.concatenate([p_top.astype(BF16), d[:, half:]], axis=1), vb)

        st = st_ref[...]
        stb = st.astype(BF16)
        o_st = _scan_pair_dot(q_all, jnp.concatenate([stb, stb], axis=0))
        o = jnp.concatenate([o_st[:, :LANES] + o_lo, o_st[:, LANES:] + o_hi], axis=0)
        st_ref[...] = st * decay + _dot_tn(vb, k_all)

        on = _rms(o, gn)
        o_ref[rows, :] = (on * g_ref[rows, :].astype(F32)).astype(BF16)

    pending = levels(0)
    for i in range(n_chunks):
        nxt = levels(i + 1) if i + 1 < n_chunks else None
        finish(*pending)
        pending = nxt


def _hgrn_scan(q, f, v, gate, g_norm, batch, seq):
    d = q.shape[-1]
    ts = _tiles()["scan"]
    q, f, v, gate = (a.reshape(batch, seq, d) for a in (q, f, v, gate))
    blk = pl.BlockSpec((None, ts, HGRN_DK), lambda b, h, s: (b, s, h))
    out = pl.pallas_call(
        _hgrn_scan_kernel,
        grid=(batch, HGRN_HEADS, seq // ts),
        in_specs=[blk, blk, blk, blk, _resident((1, HGRN_DK))],
        out_specs=blk,
        out_shape=jax.ShapeDtypeStruct((batch, seq, d), BF16),
        scratch_shapes=[pltpu.VMEM((HGRN_DK, HGRN_DK), F32)],
        compiler_params=_params(("arbitrary", "arbitrary", "arbitrary")),
        name="hgrn_scan",
    )(q, f, v, gate, g_norm)
    return out.reshape(batch * seq, d)


def _resid_mlp_kernel(h_ref, a_ref, wo_ref, gain_ref, wup_ref, wdn_ref, fin_ref, o_ref, *, final_norm):
    tf = _tiles()["ff"]
    h1 = h_ref[...] + _dot(a_ref[...], wo_ref[...])
    xb = _rms(h1, gain_ref[...]).astype(BF16)
    acc = h1
    for c0 in range(0, D_FF, tf):
        up = jnp.maximum(_dot(xb, wup_ref[:, c0:c0 + tf]), 0.0)
        acc = acc + _dot((up * up).astype(BF16), wdn_ref[c0:c0 + tf, :])
    if final_norm:
        acc = _rms(acc, fin_ref[...])
    o_ref[...] = acc


def _resid_mlp(h, a, wo, gain, wup, wdn, fin, final_norm):
    n, d = h.shape
    ka = a.shape[1]
    tm = _tiles()["mlp"]
    tok = lambda i: (i, 0)
    return pl.pallas_call(
        functools.partial(_resid_mlp_kernel, final_norm=final_norm),
        grid=(n // tm,),
        in_specs=[
            pl.BlockSpec((tm, d), tok),
            pl.BlockSpec((tm, ka), tok),
            _resident(wo.shape),
            _resident((1, d)),
            _resident(wup.shape),
            _resident(wdn.shape),
            _resident((1, d)),
        ],
        out_specs=pl.BlockSpec((tm, d), tok),
        out_shape=jax.ShapeDtypeStruct((n, d), F32),
        compiler_params=_params(("arbitrary",)),
        name="resid_mlp_final" if final_norm else "resid_mlp",
    )(h, a, wo, gain, wup, wdn, fin)


def _mla_proj_kernel(h_ref, cs_ref, gkv_ref, wdkv_ref, gc_ref, wuk_ref, wuv_ref,
                     gq_ref, wdq_ref, gcq_ref, wuqn_ref, wuqr_ref,
                     qn_ref, qr_ref, kn_ref, kr_ref, vt_ref, *, scale):
    tf = _tiles()["ff"]
    h = h_ref[...]
    hn = h * lax.rsqrt(jnp.mean(h * h, axis=-1, keepdims=True) + EPS)
    cs = cs_ref[...]
    lane = lax.broadcasted_iota(jnp.int32, cs.shape, 1)

    ckr = _dot((hn * gkv_ref[...]).astype(BF16), wdkv_ref[...])
    t = ckr[:, MLA_KV_LORA:] * cs
    kr = t + pltpu.roll(t, MLA_ROPE, axis=1)
    kr_ref[...] = jnp.where(lane < MLA_ROPE, kr, 0.0).astype(BF16)
    ckv = _rms(ckr[:, :MLA_KV_LORA], gc_ref[...]).astype(BF16)
    kn_ref[...] = _dot(ckv, wuk_ref[...]).astype(BF16)
    tk = vt_ref.shape[-1]
    for kb in range(vt_ref.shape[0]):
        for r0 in range(0, vt_ref.shape[1], tf):
            vt_ref[kb, r0:r0 + tf, :] = _dot_nt(
                wuv_ref[r0:r0 + tf, :], ckv[kb * tk:(kb + 1) * tk, :]).astype(BF16)

    cq = _rms(_dot((hn * gq_ref[...]).astype(BF16), wdq_ref[...]), gcq_ref[...]).astype(BF16)
    qn_ref[...] = (_dot(cq, wuqn_ref[...]) * scale).astype(BF16)
    qr = _dot(cq, wuqr_ref[...])
    for hd in range(MLA_HEADS):
        sl = slice(hd * LANES, (hd + 1) * LANES)
        t = qr[:, sl] * cs
        qr_ref[:, sl] = ((t + pltpu.roll(t, MLA_ROPE, axis=1)) * scale).astype(BF16)


def _mla_proj(h, cs, gkv, wdkv, gc, wuk, wuv, gq, wdq, gcq, wuqn, wuqr, seq):
    n, d = h.shape
    tm = _tiles()["mla_proj"]
    tk = _tiles()["attn_k"]
    hw = MLA_HEADS * LANES
    tok = lambda i: (i, 0)
    pos = lambda i: (i % (seq // tm), 0)
    scale = float((MLA_NOPE + MLA_ROPE) ** -0.5 * math.log2(math.e))
    wide = jax.ShapeDtypeStruct((n, hw), BF16)
    return pl.pallas_call(
        functools.partial(_mla_proj_kernel, scale=scale),
        grid=(n // tm,),
        in_specs=[
            pl.BlockSpec((tm, d), tok),
            pl.BlockSpec((tm, LANES), pos),
            _resident((1, d)), _resident(wdkv.shape), _resident((1, MLA_KV_LORA)),
            _resident(wuk.shape), _resident(wuv.shape),
            _resident((1, d)), _resident(wdq.shape), _resident((1, MLA_Q_LORA)),
            _resident(wuqn.shape), _resident(wuqr.shape),
        ],
        out_specs=[
            pl.BlockSpec((tm, hw), tok), pl.BlockSpec((tm, hw), tok), pl.BlockSpec((tm, hw), tok),
            pl.BlockSpec((tm, LANES), tok),
            pl.BlockSpec((tm // tk, hw, tk), lambda i: (i, 0, 0)),
        ],
        out_shape=[wide, wide, wide, jax.ShapeDtypeStruct((n, LANES), BF16),
                   jax.ShapeDtypeStruct((n // tk, hw, tk), BF16)],
        compiler_params=_params(("arbitrary",)),
        name="mla_proj",
    )(h, cs, gkv, wdkv, gc, wuk, wuv, gq, wdq, gcq, wuqn, wuqr)


def _mla_attn_kernel(qn_ref, qr_ref, kn_ref, kr_ref, vt_ref, o_ref, m_ref, acc_ref):
    tq = qn_ref.shape[0]
    tk = vt_ref.shape[-1]
    assert tq == tk
    heads = qn_ref.shape[1] // LANES
    strips = tq // ATTN_Q_STRIP
    n_chains = heads * strips
    qi = pl.program_id(2)

    m_ref[...] = jnp.full_like(m_ref, -jnp.inf)
    acc_ref[...] = jnp.zeros_like(acc_ref)
    ones = jnp.ones((SUM_ROWS, tk), BF16)

    def chain(c):
        g, st = divmod(c, strips)
        return g, slice(g * LANES, (g + 1) * LANES), slice(st * ATTN_Q_STRIP, (st + 1) * ATTN_Q_STRIP)

    def scores(c, j, nk):
        _, sl, cols = chain(c)
        rows = pl.ds(pl.multiple_of(j * tk, tk), nk)
        k = jnp.concatenate([kn_ref[rows, sl], kr_ref[rows, :]], axis=-1)
        q = jnp.concatenate([qn_ref[cols, sl], qr_ref[cols, sl]], axis=-1)
        return _dot_nt(k, q)

    def step(j, diagonal):
        def tile_scores(c):
            s = scores(c, j, (c % strips + 1) * ATTN_Q_STRIP if diagonal else tk)
            if diagonal:
                k_pos = lax.broadcasted_iota(jnp.int32, s.shape, 0)
                q_pos = (c % strips) * ATTN_Q_STRIP + lax.broadcasted_iota(jnp.int32, s.shape, 1)
                s = jnp.where(k_pos <= q_pos, s, -jnp.inf)
            return s

        def softmax(c, s):
            g, _, cols = chain(c)
            m_prev = m_ref[g, :, cols]
            m_new = jnp.maximum(m_prev, jnp.max(s, axis=0, keepdims=True))
            m_ref[g, :, cols] = m_new
            return jnp.exp2(m_prev - m_new), jnp.exp2(s - m_new).astype(BF16)

        def accumulate(c, alpha, p):
            g, sl, cols = chain(c)
            nk = p.shape[0]
            vt1 = jnp.concatenate([vt_ref[j, sl, :nk], ones[:, :nk]], axis=0)
            acc_ref[g, :, cols] = alpha * acc_ref[g, :, cols] + _dot(vt1, p)

        s_q, p_q = {}, {}
        for t in range(n_chains + ATTN_LOOKAHEAD + ATTN_PV_DELAY):
            if t < n_chains:
                s_q[t] = tile_scores(t)
            c = t - ATTN_LOOKAHEAD
            if 0 <= c < n_chains:
                p_q[c] = softmax(c, s_q.pop(c))
            c = t - ATTN_LOOKAHEAD - ATTN_PV_DELAY
            if 0 <= c < n_chains:
                accumulate(c, *p_q.pop(c))

    lax.fori_loop(0, qi, lambda j, c: (step(j, False), c)[1], 0)
    step(qi, True)

    for g in range(heads):
        acc = acc_ref[g]
        o = acc[:MLA_V, :] / acc[MLA_V:MLA_V + 1, :]
        o_ref[:, g * LANES:(g + 1) * LANES] = o.T.astype(BF16)


def _mla_attn(qn, qr, kn, kr, vt, batch, seq):
    tq = _tiles()["attn_q"]
    tk = vt.shape[-1]
    g = ATTN_HEADS_PER_STEP
    hw = MLA_HEADS * LANES
    qn, qr, kn = (a.reshape(batch, seq, hw) for a in (qn, qr, kn))
    vt = vt.reshape(batch, seq // tk, hw, tk)
    kr = kr.reshape(batch, seq, LANES)
    q_blk = pl.BlockSpec((None, tq, g * LANES), lambda b, h, i: (b, i, h))
    kn_blk = pl.BlockSpec((None, seq, g * LANES), lambda b, h, i: (b, 0, h))
    kr_blk = pl.BlockSpec((None, seq, LANES), lambda b, h, i: (b, 0, 0))
    vt_blk = pl.BlockSpec((None, seq // tk, g * LANES, tk), lambda b, h, i: (b, 0, h, 0))
    out = pl.pallas_call(
        _mla_attn_kernel,
        grid=(batch, MLA_HEADS // g, seq // tq),
        in_specs=[q_blk, q_blk, kn_blk, kr_blk, vt_blk],
        out_specs=q_blk,
        out_shape=jax.ShapeDtypeStruct((batch, seq, hw), BF16),
        scratch_shapes=[
            pltpu.VMEM((g, 1, tq), F32), pltpu.VMEM((g, MLA_V + SUM_ROWS, tq), F32),
        ],
        compiler_params=_params(("arbitrary", "arbitrary", "arbitrary")),
        name="mla_attn",
    )(qn, qr, kn, kr, vt)
    return out.reshape(batch * seq, hw)


def _rope_partner(w):
    half = MLA_ROPE // 2
    return jnp.concatenate([-w[..., half:], w[..., :half]], axis=-1)


def _rope_table(seq):
    half = MLA_ROPE // 2
    inv_freq = ROPE_THETA ** (-jnp.arange(half, dtype=F32) / half)
    ang = jnp.arange(seq, dtype=F32)[:, None] * inv_freq[None, :]
    cos, sin = jnp.cos(ang), jnp.sin(ang)
    return jnp.concatenate([cos, cos, sin, sin], axis=-1)


def kernel(x, hgrn_norm, hgrn_w_q, hgrn_w_f, hgrn_w_i, hgrn_w_g, hgrn_g_norm, hgrn_w_o, hgrn_lb_logits, mla_norm, mla_w_dq, mla_q_norm, mla_w_uq, mla_w_o, kv_in_norm, kv_w_dkv, kv_norm, kv_w_uk, kv_w_uv, mlp_norm, mlp_w_up, mlp_w_down, final_norm):
    batch, seq, d = x.shape
    assert d == D_MODEL and hgrn_w_q.shape[0] == 1 and mla_w_dq.shape[0] == 1
    assert seq % max(_tiles().values()) == 0
    n = batch * seq
    x2 = x.reshape(n, d)
    row = lambda g: g.reshape(1, -1).astype(F32)

    w4 = jnp.stack([hgrn_w_q[0], hgrn_w_f[0], hgrn_w_i[0], hgrn_w_g[0]]).astype(BF16)
    q, f, v, gate = _hgrn_proj(x2, row(hgrn_norm[0]), hgrn_lb_logits.astype(F32), w4)
    og = _hgrn_scan(q, f, v, gate, row(hgrn_g_norm[0]), batch, seq)
    h = _resid_mlp(x2, og, hgrn_w_o[0].astype(BF16), row(mlp_norm[0]),
                   mlp_w_up[0].astype(BF16), mlp_w_down[0].astype(BF16), row(final_norm), False)

    w_rope = kv_w_dkv[:, MLA_KV_LORA:]
    wdkv = jnp.concatenate([kv_w_dkv, _rope_partner(w_rope)], axis=-1).astype(BF16)
    wuq = mla_w_uq[0].reshape(MLA_Q_LORA, MLA_HEADS, MLA_NOPE + MLA_ROPE)
    wuqn = wuq[..., :MLA_NOPE].reshape(MLA_Q_LORA, MLA_HEADS * MLA_NOPE).astype(BF16)
    wq_rope = wuq[..., MLA_NOPE:]
    wuqr = jnp.concatenate([wq_rope, _rope_partner(wq_rope)], axis=-1)
    wuqr = wuqr.reshape(MLA_Q_LORA, MLA_HEADS * LANES).astype(BF16)
    qn, qr, kn, kr, vv = _mla_proj(
        h, _rope_table(seq), row(kv_in_norm), wdkv, row(kv_norm),
        kv_w_uk.astype(BF16), kv_w_uv.T.astype(BF16),
        row(mla_norm[0]), mla_w_dq[0].astype(BF16), row(mla_q_norm[0]), wuqn, wuqr, seq)

    attn = _mla_attn(qn, qr, kn, kr, vv, batch, seq)
    out = _resid_mlp(h, attn, mla_w_o[0].astype(BF16), row(mlp_norm[1]),
                     mlp_w_up[1].astype(BF16), mlp_w_down[1].astype(BF16), row(final_norm), True)
    return out.reshape(batch, seq, d)
```

```python
import functools
import math

import jax
import jax.numpy as jnp
from jax import lax
from jax.experimental import pallas as pl
from jax.experimental.pallas import tpu as pltpu

F32 = jnp.float32
BF16 = jnp.bfloat16

D_MODEL = 1024
HGRN_HEADS = 8
HGRN_DK = 128
MLA_HEADS = 16
MLA_NOPE = 128
MLA_ROPE = 64
MLA_V = 128
MLA_Q_LORA = 256
MLA_KV_LORA = 256
ROPE_THETA = 10000.0
D_FF = 4 * D_MODEL
EPS = 1e-6

LANES = 128
VMEM_LIMIT_BYTES = 56 * 1024 * 1024

PROJ_SUBTILES = 2
SCAN_CHUNK = 256
ATTN_HEADS_PER_STEP = 8
ATTN_Q_STRIP = 256
ATTN_LOOKAHEAD = 4
ATTN_PV_DELAY = 2
ATTN_LAG_LIMIT = 60.0
SUM_ROWS = 16


def _tiles():
    return dict(proj=512, scan=1024, mlp=512, mla_proj=512, attn_q=512, attn_k=512, ff=1024)


def _rms(x, gain):
    return x * lax.rsqrt(jnp.mean(x * x, axis=-1, keepdims=True) + EPS) * gain


def _sigmoid(x):
    return 1.0 / (1.0 + jnp.exp(-x))


def _dot(a, b):
    return jnp.dot(a, b, preferred_element_type=F32)


def _dot_nt(a, b):
    return lax.dot_general(a, b, (((1,), (1,)), ((), ())), preferred_element_type=F32)


def _dot_tn(a, b):
    return lax.dot_general(a, b, (((0,), (0,)), ((), ())), preferred_element_type=F32)


def _resident(shape):
    nd = len(shape)
    return pl.BlockSpec(shape, lambda *_: (0,) * nd, pipeline_mode=pl.Buffered(1))


def _params(semantics):
    return pltpu.CompilerParams(dimension_semantics=semantics, vmem_limit_bytes=VMEM_LIMIT_BYTES)


def _hgrn_proj_kernel(x_ref, gain_ref, lb_ref, w_ref, q_ref, f_ref, v_ref, g_ref):
    logits = lb_ref[...]
    e = jnp.exp(logits - jnp.max(logits, axis=0, keepdims=True))
    lb = e[0:1, :] / jnp.sum(e, axis=0, keepdims=True)
    sub = x_ref.shape[0] // PROJ_SUBTILES
    for r0 in range(0, x_ref.shape[0], sub):
        rows = slice(r0, r0 + sub)
        xb = _rms(x_ref[rows, :], gain_ref[...]).astype(BF16)
        yq = _dot(xb, w_ref[0])
        q_ref[rows, :] = (yq * _sigmoid(yq)).astype(BF16)
        f_ref[rows, :] = lb + (1.0 - lb) * _sigmoid(_dot(xb, w_ref[1]))
        v_ref[rows, :] = _dot(xb, w_ref[2]).astype(BF16)
        yg = _dot(xb, w_ref[3])
        g_ref[rows, :] = (yg * _sigmoid(yg)).astype(BF16)


def _hgrn_proj(x2, gain, lb_logits, w4):
    n, d = x2.shape
    tm = _tiles()["proj"]
    tok = lambda i: (i, 0)
    return pl.pallas_call(
        _hgrn_proj_kernel,
        grid=(n // tm,),
        in_specs=[
            pl.BlockSpec((tm, d), tok),
            _resident((1, d)),
            _resident(lb_logits.shape),
            _resident(w4.shape),
        ],
        out_specs=[pl.BlockSpec((tm, d), tok)] * 4,
        out_shape=[
            jax.ShapeDtypeStruct((n, d), BF16),
            jax.ShapeDtypeStruct((n, d), F32),
            jax.ShapeDtypeStruct((n, d), BF16),
            jax.ShapeDtypeStruct((n, d), BF16),
        ],
        compiler_params=_params(("arbitrary",)),
        name="hgrn_proj",
    )(x2, gain, lb_logits, w4)


SUBLANES = 8


def _scan_pack_lhs(x):
    half = x.shape[0] // 2
    return jnp.concatenate([x[:half], x[half:]], axis=1)


def _scan_pack_rhs(x):
    half = x.shape[0] // 2
    z = jnp.zeros((half, x.shape[1]), x.dtype)
    return jnp.concatenate([jnp.concatenate([x[:half], z], axis=1),
                            jnp.concatenate([z, x[half:]], axis=1)], axis=0)


def _scan_pair_dot(lhs, rhs):
    return _dot_nt(_scan_pack_lhs(lhs), _scan_pack_rhs(rhs))


def _hgrn_scan_kernel(q_ref, f_ref, v_ref, g_ref, gn_ref, o_ref, st_ref):
    c = SCAN_CHUNK
    half = c // 2
    nv = c // SUBLANES
    n_chunks = q_ref.shape[0] // c

    @pl.when(pl.program_id(2) == 0)
    def _():
        st_ref[...] = jnp.zeros_like(st_ref)

    t_i = lax.broadcasted_iota(jnp.int32, (half, c), 0)
    s_i = lax.broadcasted_iota(jnp.int32, (half, c), 1) & (half - 1)
    diff = t_i ^ s_i
    level = jnp.zeros((half, c), jnp.int32)
    h = 1
    while h < half:
        level = level + (diff >= h).astype(jnp.int32)
        h *= 2
    level = jnp.where(s_i <= t_i, level, -1)
    sub = lax.broadcasted_iota(jnp.int32, (SUBLANES, LANES), 0)
    zero = jnp.zeros((SUBLANES, LANES), F32)
    gn = gn_ref[...]

    def tiles(x):
        return [x[SUBLANES * a:SUBLANES * (a + 1)] for a in range(nv)]

    def cat(pieces):
        return jnp.concatenate(pieces, axis=0).astype(BF16)

    def levels(i):
        rows = pl.ds(i * c, c)
        f = f_ref[rows, :]
        qb = q_ref[rows, :]
        vb = v_ref[rows, :]
        fv = tiles(f)
        kv = [1.0 - x for x in fv]
        qv = tiles(qb.astype(F32))

        sc = jnp.where(level == 0, _scan_pair_dot(qb, cat(kv)), 0.0)
        lvl = 1
        g_pre = list(fv)
        h_suf = [None] * nv
        tot = list(fv)

        h = 1
        while h < SUBLANES:
            qt = [qv[a] * g_pre[a] for a in range(nv)]
            kt = [kv[a] if h_suf[a] is None else kv[a] * h_suf[a] for a in range(nv)]
            sc = jnp.where(level == lvl, _scan_pair_dot(cat(qt), cat(kt)), sc)
            lvl += 1
            upper = (sub & h) != 0
            for a in range(nv):
                up = pltpu.roll(tot[a], h, axis=0)
                dn = up if 2 * h == SUBLANES else pltpu.roll(tot[a], SUBLANES - h, axis=0)
                g_pre[a] = g_pre[a] * jnp.where(upper, up, 1.0)
                hs = jnp.where(upper, 1.0, dn)
                h_suf[a] = hs if h_suf[a] is None else h_suf[a] * hs
                tot[a] = tot[a] * (up if 2 * h == SUBLANES else jnp.where(upper, up, dn))
            h *= 2

        p_top = None
        m = 1
        while m < nv:
            is_upper = [(a // m) % 2 == 1 for a in range(nv)]
            qt = [qv[a] * g_pre[a] if is_upper[a] else zero for a in range(nv)]
            kt = [zero if is_upper[a] else kv[a] * h_suf[a] for a in range(nv)]
            if 4 * m < nv:
                sc = jnp.where(level == lvl, _scan_pair_dot(cat(qt), cat(kt)), sc)
                lvl += 1
            elif 2 * m < nv:
                sc = sc + _scan_pair_dot(cat(qt), cat(kt))
                lvl += 1
            else:
                p_top = _dot_nt(cat(qt[nv // 2:]), cat(kt[:nv // 2]))
            for b in range(0, nv, 2 * m):
                t_lo, t_hi = tot[b], tot[b + m]
                for a in range(b + m, b + 2 * m):
                    g_pre[a] = g_pre[a] * t_lo
                for a in range(b, b + m):
                    h_suf[a] = h_suf[a] * t_hi
                t_all = t_lo * t_hi
                for a in range(b, b + 2 * m):
                    tot[a] = t_all
            m *= 2

        q_all = cat([qv[a] * g_pre[a] for a in range(nv)])
        k_all = cat([kv[a] * h_suf[a] for a in range(nv)])
        return rows, vb, sc, p_top, q_all, k_all, tot[0][0:1, :]

    def finish(rows, vb, sc, p_top, q_all, k_all, decay):
        d = sc.astype(BF16)
        o_lo = _dot(d[:, :half], vb[:half])
        o_hi = _dot(jnp.concatenate([p_top.astype(BF16), d[:, half:]], axis=1), vb)

        st = st_ref[...]
        stb = st.astype(BF16)
        o_st = _scan_pair_dot(q_all, jnp.concatenate([stb, stb], axis=0))
        o = jnp.concatenate([o_st[:, :LANES] + o_lo, o_st[:, LANES:] + o_hi], axis=0)
        st_ref[...] = st * decay + _dot_tn(vb, k_all)

        on = _rms(o, gn)
        o_ref[rows, :] = (on * g_ref[rows, :].astype(F32)).astype(BF16)

    pending = levels(0)
    for i in range(n_chunks):
        nxt = levels(i + 1) if i + 1 < n_chunks else None
        finish(*pending)
        pending = nxt


def _hgrn_scan(q, f, v, gate, g_norm, batch, seq):
    d = q.shape[-1]
    ts = _tiles()["scan"]
    q, f, v, gate = (a.reshape(batch, seq, d) for a in (q, f, v, gate))
    blk = pl.BlockSpec((None, ts, HGRN_DK), lambda b, h, s: (b, s, h))
    out = pl.pallas_call(
        _hgrn_scan_kernel,
        grid=(batch, HGRN_HEADS, seq // ts),
        in_specs=[blk, blk, blk, blk, _resident((1, HGRN_DK))],
        out_specs=blk,
        out_shape=jax.ShapeDtypeStruct((batch, seq, d), BF16),
        scratch_shapes=[pltpu.VMEM((HGRN_DK, HGRN_DK), F32)],
        compiler_params=_params(("arbitrary", "arbitrary", "arbitrary")),
        name="hgrn_scan",
    )(q, f, v, gate, g_norm)
    return out.reshape(batch * seq, d)


def _resid_mlp_kernel(h_ref, a_ref, wo_ref, gain_ref, wup_ref, wdn_ref, fin_ref, o_ref, *, final_norm):
    tf = _tiles()["ff"]
    h1 = h_ref[...] + _dot(a_ref[...], wo_ref[...])
    xb = _rms(h1, gain_ref[...]).astype(BF16)
    acc = h1
    for c0 in range(0, D_FF, tf):
        up = jnp.maximum(_dot(xb, wup_ref[:, c0:c0 + tf]), 0.0)
        acc = acc + _dot((up * up).astype(BF16), wdn_ref[c0:c0 + tf, :])
    if final_norm:
        acc = _rms(acc, fin_ref[...])
    o_ref[...] = acc


def _resid_mlp(h, a, wo, gain, wup, wdn, fin, final_norm):
    n, d = h.shape
    ka = a.shape[1]
    tm = _tiles()["mlp"]
    tok = lambda i: (i, 0)
    return pl.pallas_call(
        functools.partial(_resid_mlp_kernel, final_norm=final_norm),
        grid=(n // tm,),
        in_specs=[
            pl.BlockSpec((tm, d), tok),
            pl.BlockSpec((tm, ka), tok),
            _resident(wo.shape),
            _resident((1, d)),
            _resident(wup.shape),
            _resident(wdn.shape),
            _resident((1, d)),
        ],
        out_specs=pl.BlockSpec((tm, d), tok),
        out_shape=jax.ShapeDtypeStruct((n, d), F32),
        compiler_params=_params(("arbitrary",)),
        name="resid_mlp_final" if final_norm else "resid_mlp",
    )(h, a, wo, gain, wup, wdn, fin)


def _mla_proj_kernel(h_ref, cs_ref, gkv_ref, wdkv_ref, gc_ref, wuk_ref, wuv_ref,
                     gq_ref, wdq_ref, gcq_ref, wuqn_ref, wuqr_ref,
                     qn_ref, qr_ref, kn_ref, kr_ref, vt_ref, *, scale):
    tf = _tiles()["ff"]
    tk = vt_ref.shape[-1]
    sub = h_ref.shape[0] // PROJ_SUBTILES
    assert tk % sub == 0
    lane = lax.broadcasted_iota(jnp.int32, (sub, LANES), 1)
    for r0 in range(0, h_ref.shape[0], sub):
        rows = slice(r0, r0 + sub)
        h = h_ref[rows, :]
        hn = h * lax.rsqrt(jnp.mean(h * h, axis=-1, keepdims=True) + EPS)
        cs = cs_ref[rows, :]

        ckr = _dot((hn * gkv_ref[...]).astype(BF16), wdkv_ref[...])
        cq = _dot((hn * gq_ref[...]).astype(BF16), wdq_ref[...])
        ckv = _rms(ckr[:, :MLA_KV_LORA], gc_ref[...]).astype(BF16)
        cq = _rms(cq, gcq_ref[...]).astype(BF16)
        qr = _dot(cq, wuqr_ref[...])
        for hd in range(MLA_HEADS):
            sl = slice(hd * LANES, (hd + 1) * LANES)
            t = qr[:, sl] * cs
            qr_ref[rows, sl] = ((t + pltpu.roll(t, MLA_ROPE, axis=1)) * scale).astype(BF16)
        t = ckr[:, MLA_KV_LORA:] * cs
        kr = t + pltpu.roll(t, MLA_ROPE, axis=1)
        kr_ref[rows, :] = jnp.where(lane < MLA_ROPE, kr, 0.0).astype(BF16)
        kn_ref[rows, :] = _dot(ckv, wuk_ref[...]).astype(BF16)
        kb, c0 = divmod(r0, tk)
        for f0 in range(0, vt_ref.shape[1], tf):
            vt_ref[kb, f0:f0 + tf, c0:c0 + sub] = _dot_nt(wuv_ref[f0:f0 + tf, :], ckv).astype(BF16)
        qn_ref[rows, :] = (_dot(cq, wuqn_ref[...]) * scale).astype(BF16)


def _mla_proj(h, cs, gkv, wdkv, gc, wuk, wuv, gq, wdq, gcq, wuqn, wuqr, seq):
    n, d = h.shape
    tm = _tiles()["mla_proj"]
    tk = _tiles()["attn_k"]
    hw = MLA_HEADS * LANES
    tok = lambda i: (i, 0)
    pos = lambda i: (i % (seq // tm), 0)
    scale = float((MLA_NOPE + MLA_ROPE) ** -0.5 * math.log2(math.e))
    wide = jax.ShapeDtypeStruct((n, hw), BF16)
    return pl.pallas_call(
        functools.partial(_mla_proj_kernel, scale=scale),
        grid=(n // tm,),
        in_specs=[
            pl.BlockSpec((tm, d), tok),
            pl.BlockSpec((tm, LANES), pos),
            _resident((1, d)), _resident(wdkv.shape), _resident((1, MLA_KV_LORA)),
            _resident(wuk.shape), _resident(wuv.shape),
            _resident((1, d)), _resident(wdq.shape), _resident((1, MLA_Q_LORA)),
            _resident(wuqn.shape), _resident(wuqr.shape),
        ],
        out_specs=[
            pl.BlockSpec((tm, hw), tok), pl.BlockSpec((tm, hw), tok), pl.BlockSpec((tm, hw), tok),
            pl.BlockSpec((tm, LANES), tok),
            pl.BlockSpec((tm // tk, hw, tk), lambda i: (i, 0, 0)),
        ],
        out_shape=[wide, wide, wide, jax.ShapeDtypeStruct((n, LANES), BF16),
                   jax.ShapeDtypeStruct((n // tk, hw, tk), BF16)],
        compiler_params=_params(("arbitrary",)),
        name="mla_proj",
    )(h, cs, gkv, wdkv, gc, wuk, wuv, gq, wdq, gcq, wuqn, wuqr)


def _mla_attn_kernel(qn_ref, qr_ref, kn_ref, kr_ref, vt_ref, o_ref, m_ref, acc_ref, gap_ref):
    tq = qn_ref.shape[0]
    tk = vt_ref.shape[-1]
    assert tq == tk
    heads = qn_ref.shape[1] // LANES
    strips = tq // ATTN_Q_STRIP
    n_chains = heads * strips
    qi = pl.program_id(2)
    ones = jnp.ones((SUM_ROWS, tk), BF16)

    def chain(c):
        g, st = divmod(c, strips)
        return g, slice(g * LANES, (g + 1) * LANES), slice(st * ATTN_Q_STRIP, (st + 1) * ATTN_Q_STRIP)

    def reset():
        m_ref[...] = jnp.full_like(m_ref, -jnp.inf)
        acc_ref[...] = jnp.zeros_like(acc_ref)

    def scores(c, j, nk):
        _, sl, cols = chain(c)
        rows = pl.ds(pl.multiple_of(j * tk, tk), nk)
        k = jnp.concatenate([kn_ref[rows, sl], kr_ref[rows, :]], axis=-1)
        q = jnp.concatenate([qn_ref[cols, sl], qr_ref[cols, sl]], axis=-1)
        return _dot_nt(k, q)

    def step(j, diagonal, lagged):
        def tile_scores(c):
            s = scores(c, j, (c % strips + 1) * ATTN_Q_STRIP if diagonal else tk)
            if diagonal:
                k_pos = lax.broadcasted_iota(jnp.int32, s.shape, 0)
                q_pos = (c % strips) * ATTN_Q_STRIP + lax.broadcasted_iota(jnp.int32, s.shape, 1)
                s = jnp.where(k_pos <= q_pos, s, -jnp.inf)
            return s

        def softmax(c, s):
            g, _, cols = chain(c)
            m_prev = m_ref[g, :, cols]
            s_max = jnp.max(s, axis=0, keepdims=True)
            if lagged:
                m_prev = jnp.where(j == 0, s[0:1, :], m_prev)
                gap_ref[g, :, cols] = jnp.maximum(gap_ref[g, :, cols], s_max - m_prev)
            m_new = jnp.maximum(m_prev, s_max)
            m_ref[g, :, cols] = m_new
            p = jnp.exp2(s - (m_prev if lagged else m_new)).astype(BF16)
            return jnp.exp2(m_prev - m_new), p

        def accumulate(c, alpha, p):
            g, sl, cols = chain(c)
            nk = p.shape[0]
            vt1 = jnp.concatenate([vt_ref[j, sl, :nk], ones[:, :nk]], axis=0)
            pv = _dot(vt1, p)
            if lagged:
                acc_ref[g, :, cols] = alpha * (acc_ref[g, :, cols] + pv)
            else:
                acc_ref[g, :, cols] = alpha * acc_ref[g, :, cols] + pv

        s_q, p_q = {}, {}
        for t in range(n_chains + ATTN_LOOKAHEAD + ATTN_PV_DELAY):
            if t < n_chains:
                s_q[t] = tile_scores(t)
            c = t - ATTN_LOOKAHEAD
            if 0 <= c < n_chains:
                p_q[c] = softmax(c, s_q.pop(c))
            c = t - ATTN_LOOKAHEAD - ATTN_PV_DELAY
            if 0 <= c < n_chains:
                accumulate(c, *p_q.pop(c))

    def sweep(lagged):
        lax.fori_loop(0, qi, lambda j, c: (step(j, False, lagged), c)[1], 0)
        step(qi, True, lagged)

    reset()
    gap_ref[...] = jnp.zeros_like(gap_ref)
    sweep(lagged=True)

    @pl.when(jnp.max(gap_ref[...]) > ATTN_LAG_LIMIT)
    def _():
        reset()
        sweep(lagged=False)

    for g in range(heads):
        acc = acc_ref[g]
        o = acc[:MLA_V, :] / acc[MLA_V:MLA_V + 1, :]
        o_ref[:, g * LANES:(g + 1) * LANES] = o.T.astype(BF16)


def _mla_attn(qn, qr, kn, kr, vt, batch, seq):
    tq = _tiles()["attn_q"]
    tk = vt.shape[-1]
    g = ATTN_HEADS_PER_STEP
    hw = MLA_HEADS * LANES
    qn, qr, kn = (a.reshape(batch, seq, hw) for a in (qn, qr, kn))
    vt = vt.reshape(batch, seq // tk, hw, tk)
    kr = kr.reshape(batch, seq, LANES)
    q_blk = pl.BlockSpec((None, tq, g * LANES), lambda b, h, i: (b, i, h))
    kn_blk = pl.BlockSpec((None, seq, g * LANES), lambda b, h, i: (b, 0, h))
    kr_blk = pl.BlockSpec((None, seq, LANES), lambda b, h, i: (b, 0, 0))
    vt_blk = pl.BlockSpec((None, seq // tk, g * LANES, tk), lambda b, h, i: (b, 0, h, 0))
    out = pl.pallas_call(
        _mla_attn_kernel,
        grid=(batch, MLA_HEADS // g, seq // tq),
        in_specs=[q_blk, q_blk, kn_blk, kr_blk, vt_blk],
        out_specs=q_blk,
        out_shape=jax.ShapeDtypeStruct((batch, seq, hw), BF16),
        scratch_shapes=[
            pltpu.VMEM((g, 1, tq), F32), pltpu.VMEM((g, MLA_V + SUM_ROWS, tq), F32),
            pltpu.VMEM((g, 1, tq), F32),
        ],
        compiler_params=_params(("arbitrary", "arbitrary", "arbitrary")),
        name="mla_attn",
    )(qn, qr, kn, kr, vt)
    return out.reshape(batch * seq, hw)


def _rope_partner(w):
    half = MLA_ROPE // 2
    return jnp.concatenate([-w[..., half:], w[..., :half]], axis=-1)


def _rope_table(seq):
    half = MLA_ROPE // 2
    inv_freq = ROPE_THETA ** (-jnp.arange(half, dtype=F32) / half)
    ang = jnp.arange(seq, dtype=F32)[:, None] * inv_freq[None, :]
    cos, sin = jnp.cos(ang), jnp.sin(ang)
    return jnp.concatenate([cos, cos, sin, sin], axis=-1)


def kernel(x, hgrn_norm, hgrn_w_q, hgrn_w_f, hgrn_w_i, hgrn_w_g, hgrn_g_norm, hgrn_w_o, hgrn_lb_logits, mla_norm, mla_w_dq, mla_q_norm, mla_w_uq, mla_w_o, kv_in_norm, kv_w_dkv, kv_norm, kv_w_uk, kv_w_uv, mlp_norm, mlp_w_up, mlp_w_down, final_norm):
    batch, seq, d = x.shape
    assert d == D_MODEL and hgrn_w_q.shape[0] == 1 and mla_w_dq.shape[0] == 1
    assert seq % max(_tiles().values()) == 0
    n = batch * seq
    x2 = x.reshape(n, d)
    row = lambda g: g.reshape(1, -1).astype(F32)

    w4 = jnp.stack([hgrn_w_q[0], hgrn_w_f[0], hgrn_w_i[0], hgrn_w_g[0]]).astype(BF16)
    q, f, v, gate = _hgrn_proj(x2, row(hgrn_norm[0]), hgrn_lb_logits.astype(F32), w4)
    og = _hgrn_scan(q, f, v, gate, row(hgrn_g_norm[0]), batch, seq)
    h = _resid_mlp(x2, og, hgrn_w_o[0].astype(BF16), row(mlp_norm[0]),
                   mlp_w_up[0].astype(BF16), mlp_w_down[0].astype(BF16), row(final_norm), False)

    w_rope = kv_w_dkv[:, MLA_KV_LORA:]
    wdkv = jnp.concatenate([kv_w_dkv, _rope_partner(w_rope)], axis=-1).astype(BF16)
    wuq = mla_w_uq[0].reshape(MLA_Q_LORA, MLA_HEADS, MLA_NOPE + MLA_ROPE)
    wuqn = wuq[..., :MLA_NOPE].reshape(MLA_Q_LORA, MLA_HEADS * MLA_NOPE).astype(BF16)
    wq_rope = wuq[..., MLA_NOPE:]
    wuqr = jnp.concatenate([wq_rope, _rope_partner(wq_rope)], axis=-1)
    wuqr = wuqr.reshape(MLA_Q_LORA, MLA_HEADS * LANES).astype(BF16)
    qn, qr, kn, kr, vv = _mla_proj(
        h, _rope_table(seq), row(kv_in_norm), wdkv, row(kv_norm),
        kv_w_uk.astype(BF16), kv_w_uv.T.astype(BF16),
        row(mla_norm[0]), mla_w_dq[0].astype(BF16), row(mla_q_norm[0]), wuqn, wuqr, seq)

    attn = _mla_attn(qn, qr, kn, kr, vv, batch, seq)
    out = _resid_mlp(h, attn, mla_w_o[0].astype(BF16), row(mlp_norm[1]),
                     mlp_w_up[1].astype(BF16), mlp_w_down[1].astype(BF16), row(final_norm), True)
    return out.reshape(batch, seq, d)
```

```python
import functools
import math

import jax
import jax.numpy as jnp
from jax import lax
from jax.experimental import pallas as pl
from jax.experimental.pallas import tpu as pltpu

F32 = jnp.float32
BF16 = jnp.bfloat16

D_MODEL = 1024
HGRN_HEADS = 8
HGRN_DK = 128
MLA_HEADS = 16
MLA_NOPE = 128
MLA_ROPE = 64
MLA_V = 128
MLA_Q_LORA = 256
MLA_KV_LORA = 256
ROPE_THETA = 10000.0
D_FF = 4 * D_MODEL
EPS = 1e-6

LANES = 128
VMEM_LIMIT_BYTES = 56 * 1024 * 1024

PROJ_SUBTILES = 2
SCAN_CHUNK = 256
ATTN_HEADS_PER_STEP = 8
ATTN_Q_STRIP = 256
ATTN_LOOKAHEAD = 4
ATTN_PV_DELAY = 2
ATTN_LAG_LIMIT = 60.0
SUM_ROWS = 16


def _tiles():
    return dict(proj=512, scan=4096, mlp=512, mla_proj=512, attn_q=512, attn_k=512, ff=1024)


def _rms(x, gain):
    return x * lax.rsqrt(jnp.mean(x * x, axis=-1, keepdims=True) + EPS) * gain


def _sigmoid(x):
    return 1.0 / (1.0 + jnp.exp(-x))


def _dot(a, b):
    return jnp.dot(a, b, preferred_element_type=F32)


def _dot_nt(a, b):
    return lax.dot_general(a, b, (((1,), (1,)), ((), ())), preferred_element_type=F32)


def _dot_tn(a, b):
    return lax.dot_general(a, b, (((0,), (0,)), ((), ())), preferred_element_type=F32)


def _resident(shape):
    nd = len(shape)
    return pl.BlockSpec(shape, lambda *_: (0,) * nd, pipeline_mode=pl.Buffered(1))


def _params(semantics):
    return pltpu.CompilerParams(dimension_semantics=semantics, vmem_limit_bytes=VMEM_LIMIT_BYTES)


def _hgrn_proj_kernel(x_ref, gain_ref, lb_ref, w_ref, q_ref, f_ref, v_ref, g_ref):
    logits = lb_ref[...]
    e = jnp.exp(logits - jnp.max(logits, axis=0, keepdims=True))
    lb = e[0:1, :] / jnp.sum(e, axis=0, keepdims=True)
    sub = x_ref.shape[0] // PROJ_SUBTILES
    for r0 in range(0, x_ref.shape[0], sub):
        rows = slice(r0, r0 + sub)
        xb = _rms(x_ref[rows, :], gain_ref[...]).astype(BF16)
        yq = _dot(xb, w_ref[0])
        q_ref[rows, :] = (yq * _sigmoid(yq)).astype(BF16)
        f_ref[rows, :] = lb + (1.0 - lb) * _sigmoid(_dot(xb, w_ref[1]))
        v_ref[rows, :] = _dot(xb, w_ref[2]).astype(BF16)
        yg = _dot(xb, w_ref[3])
        g_ref[rows, :] = (yg * _sigmoid(yg)).astype(BF16)


def _hgrn_proj(x2, gain, lb_logits, w4):
    n, d = x2.shape
    tm = _tiles()["proj"]
    tok = lambda i: (i, 0)
    return pl.pallas_call(
        _hgrn_proj_kernel,
        grid=(n // tm,),
        in_specs=[
            pl.BlockSpec((tm, d), tok),
            _resident((1, d)),
            _resident(lb_logits.shape),
            _resident(w4.shape),
        ],
        out_specs=[pl.BlockSpec((tm, d), tok)] * 4,
        out_shape=[
            jax.ShapeDtypeStruct((n, d), BF16),
            jax.ShapeDtypeStruct((n, d), F32),
            jax.ShapeDtypeStruct((n, d), BF16),
            jax.ShapeDtypeStruct((n, d), BF16),
        ],
        compiler_params=_params(("arbitrary",)),
        name="hgrn_proj",
    )(x2, gain, lb_logits, w4)


SUBLANES = 8


def _scan_pack_lhs(x):
    half = x.shape[0] // 2
    return jnp.concatenate([x[:half], x[half:]], axis=1)


def _scan_pack_rhs(x):
    half = x.shape[0] // 2
    z = jnp.zeros((half, x.shape[1]), x.dtype)
    return jnp.concatenate([jnp.concatenate([x[:half], z], axis=1),
                            jnp.concatenate([z, x[half:]], axis=1)], axis=0)


def _scan_pair_dot(lhs, rhs):
    return _dot_nt(_scan_pack_lhs(lhs), _scan_pack_rhs(rhs))


def _hgrn_scan_kernel(q_ref, f_ref, v_ref, g_ref, gn_ref, o_ref, st_ref):
    c = SCAN_CHUNK
    half = c // 2
    nv = c // SUBLANES
    n_chunks = q_ref.shape[0] // c

    @pl.when(pl.program_id(2) == 0)
    def _():
        st_ref[...] = jnp.zeros_like(st_ref)

    t_i = lax.broadcasted_iota(jnp.int32, (half, c), 0)
    s_i = lax.broadcasted_iota(jnp.int32, (half, c), 1) & (half - 1)
    diff = t_i ^ s_i
    level = jnp.zeros((half, c), jnp.int32)
    h = 1
    while h < half:
        level = level + (diff >= h).astype(jnp.int32)
        h *= 2
    level = jnp.where(s_i <= t_i, level, -1)
    sub = lax.broadcasted_iota(jnp.int32, (SUBLANES, LANES), 0)
    zero = jnp.zeros((SUBLANES, LANES), F32)
    gn = gn_ref[...]

    def tiles(x):
        return [x[SUBLANES * a:SUBLANES * (a + 1)] for a in range(nv)]

    def cat(pieces):
        return jnp.concatenate(pieces, axis=0).astype(BF16)

    def levels(i):
        rows = pl.ds(i * c, c)
        f = f_ref[rows, :]
        qb = q_ref[rows, :]
        vb = v_ref[rows, :]
        fv = tiles(f)
        kv = [1.0 - x for x in fv]
        qv = tiles(qb.astype(F32))

        sc = jnp.where(level == 0, _scan_pair_dot(qb, cat(kv)), 0.0)
        lvl = 1
        g_pre = list(fv)
        h_suf = [None] * nv
        tot = list(fv)

        h = 1
        while h < SUBLANES:
            qt = [qv[a] * g_pre[a] for a in range(nv)]
            kt = [kv[a] if h_suf[a] is None else kv[a] * h_suf[a] for a in range(nv)]
            sc = jnp.where(level == lvl, _scan_pair_dot(cat(qt), cat(kt)), sc)
            lvl += 1
            upper = (sub & h) != 0
            for a in range(nv):
                up = pltpu.roll(tot[a], h, axis=0)
                dn = up if 2 * h == SUBLANES else pltpu.roll(tot[a], SUBLANES - h, axis=0)
                g_pre[a] = g_pre[a] * jnp.where(upper, up, 1.0)
                hs = jnp.where(upper, 1.0, dn)
                h_suf[a] = hs if h_suf[a] is None else h_suf[a] * hs
                tot[a] = tot[a] * (up if 2 * h == SUBLANES else jnp.where(upper, up, dn))
            h *= 2

        p_top = None
        m = 1
        while m < nv:
            is_upper = [(a // m) % 2 == 1 for a in range(nv)]
            qt = [qv[a] * g_pre[a] if is_upper[a] else zero for a in range(nv)]
            kt = [zero if is_upper[a] else kv[a] * h_suf[a] for a in range(nv)]
            if 4 * m < nv:
                sc = jnp.where(level == lvl, _scan_pair_dot(cat(qt), cat(kt)), sc)
                lvl += 1
            elif 2 * m < nv:
                sc = sc + _scan_pair_dot(cat(qt), cat(kt))
                lvl += 1
            else:
                p_top = _dot_nt(cat(qt[nv // 2:]), cat(kt[:nv // 2]))
            for b in range(0, nv, 2 * m):
                t_lo, t_hi = tot[b], tot[b + m]
                for a in range(b + m, b + 2 * m):
                    g_pre[a] = g_pre[a] * t_lo
                for a in range(b, b + m):
                    h_suf[a] = h_suf[a] * t_hi
                t_all = t_lo * t_hi
                for a in range(b, b + 2 * m):
                    tot[a] = t_all
            m *= 2

        q_all = cat([qv[a] * g_pre[a] for a in range(nv)])
        k_all = cat([kv[a] * h_suf[a] for a in range(nv)])
        return rows, vb, sc, p_top, q_all, k_all, tot[0][0:1, :]

    def finish(rows, vb, sc, p_top, q_all, k_all, decay):
        d = sc.astype(BF16)
        o_lo = _dot(d[:, :half], vb[:half])
        o_hi = _dot(jnp.concatenate([p_top.astype(BF16), d[:, half:]], axis=1), vb)

        st = st_ref[...]
        stb = st.astype(BF16)
        o_st = _scan_pair_dot(q_all, jnp.concatenate([stb, stb], axis=0))
        o = jnp.concatenate([o_st[:, :LANES] + o_lo, o_st[:, LANES:] + o_hi], axis=0)
        st_ref[...] = st * decay + _dot_tn(vb, k_all)

        on = _rms(o, gn)
        o_ref[rows, :] = (on * g_ref[rows, :].astype(F32)).astype(BF16)

    pending = levels(0)
    for i in range(n_chunks):
        nxt = levels(i + 1) if i + 1 < n_chunks else None
        finish(*pending)
        pending = nxt


def _hgrn_scan(q, f, v, gate, g_norm, batch, seq):
    d = q.shape[-1]
    ts = _tiles()["scan"]
    q, f, v, gate = (a.reshape(batch, seq, d) for a in (q, f, v, gate))
    blk = pl.BlockSpec((None, ts, HGRN_DK), lambda b, h, s: (b, s, h))
    out = pl.pallas_call(
        _hgrn_scan_kernel,
        grid=(batch, HGRN_HEADS, seq // ts),
        in_specs=[blk, blk, blk, blk, _resident((1, HGRN_DK))],
        out_specs=blk,
        out_shape=jax.ShapeDtypeStruct((batch, seq, d), BF16),
        scratch_shapes=[pltpu.VMEM((HGRN_DK, HGRN_DK), F32)],
        compiler_params=_params(("arbitrary", "arbitrary", "arbitrary")),
        name="hgrn_scan",
    )(q, f, v, gate, g_norm)
    return out.reshape(batch * seq, d)


def _resid_mlp_kernel(h_ref, a_ref, wo_ref, gain_ref, wup_ref, wdn_ref, fin_ref, o_ref, *, final_norm):
    tf = _tiles()["ff"]
    h1 = h_ref[...] + _dot(a_ref[...], wo_ref[...])
    xb = _rms(h1, gain_ref[...]).astype(BF16)
    acc = h1
    for c0 in range(0, D_FF, tf):
        up = jnp.maximum(_dot(xb, wup_ref[:, c0:c0 + tf]), 0.0)
        acc = acc + _dot((up * up).astype(BF16), wdn_ref[c0:c0 + tf, :])
    if final_norm:
        acc = _rms(acc, fin_ref[...])
    o_ref[...] = acc


def _resid_mlp(h, a, wo, gain, wup, wdn, fin, final_norm):
    n, d = h.shape
    ka = a.shape[1]
    tm = _tiles()["mlp"]
    tok = lambda i: (i, 0)
    return pl.pallas_call(
        functools.partial(_resid_mlp_kernel, final_norm=final_norm),
        grid=(n // tm,),
        in_specs=[
            pl.BlockSpec((tm, d), tok),
            pl.BlockSpec((tm, ka), tok),
            _resident(wo.shape),
            _resident((1, d)),
            _resident(wup.shape),
            _resident(wdn.shape),
            _resident((1, d)),
        ],
        out_specs=pl.BlockSpec((tm, d), tok),
        out_shape=jax.ShapeDtypeStruct((n, d), F32),
        compiler_params=_params(("arbitrary",)),
        name="resid_mlp_final" if final_norm else "resid_mlp",
    )(h, a, wo, gain, wup, wdn, fin)


def _mla_proj_kernel(h_ref, cs_ref, gkv_ref, wdkv_ref, gc_ref, wuk_ref, wuv_ref,
                     gq_ref, wdq_ref, gcq_ref, wuqn_ref, wuqr_ref,
                     qn_ref, qr_ref, kn_ref, kr_ref, vt_ref, *, scale):
    tf = _tiles()["ff"]
    tk = vt_ref.shape[-1]
    sub = h_ref.shape[0] // PROJ_SUBTILES
    assert tk % sub == 0
    lane = lax.broadcasted_iota(jnp.int32, (sub, LANES), 1)
    for r0 in range(0, h_ref.shape[0], sub):
        rows = slice(r0, r0 + sub)
        h = h_ref[rows, :]
        hn = h * lax.rsqrt(jnp.mean(h * h, axis=-1, keepdims=True) + EPS)
        cs = cs_ref[rows, :]

        ckr = _dot((hn * gkv_ref[...]).astype(BF16), wdkv_ref[...])
        cq = _dot((hn * gq_ref[...]).astype(BF16), wdq_ref[...])
        ckv = _rms(ckr[:, :MLA_KV_LORA], gc_ref[...]).astype(BF16)
        cq = _rms(cq, gcq_ref[...]).astype(BF16)
        qr = _dot(cq, wuqr_ref[...])
        for hd in range(MLA_HEADS):
            sl = slice(hd * LANES, (hd + 1) * LANES)
            t = qr[:, sl] * cs
            qr_ref[rows, sl] = ((t + pltpu.roll(t, MLA_ROPE, axis=1)) * scale).astype(BF16)
        t = ckr[:, MLA_KV_LORA:] * cs
        kr = t + pltpu.roll(t, MLA_ROPE, axis=1)
        kr_ref[rows, :] = jnp.where(lane < MLA_ROPE, kr, 0.0).astype(BF16)
        kn_ref[rows, :] = _dot(ckv, wuk_ref[...]).astype(BF16)
        kb, c0 = divmod(r0, tk)
        for f0 in range(0, vt_ref.shape[1], tf):
            vt_ref[kb, f0:f0 + tf, c0:c0 + sub] = _dot_nt(wuv_ref[f0:f0 + tf, :], ckv).astype(BF16)
        qn_ref[rows, :] = (_dot(cq, wuqn_ref[...]) * scale).astype(BF16)


def _mla_proj(h, cs, gkv, wdkv, gc, wuk, wuv, gq, wdq, gcq, wuqn, wuqr, seq):
    n, d = h.shape
    tm = _tiles()["mla_proj"]
    tk = _tiles()["attn_k"]
    hw = MLA_HEADS * LANES
    tok = lambda i: (i, 0)
    pos = lambda i: (i % (seq // tm), 0)
    scale = float((MLA_NOPE + MLA_ROPE) ** -0.5 * math.log2(math.e))
    wide = jax.ShapeDtypeStruct((n, hw), BF16)
    return pl.pallas_call(
        functools.partial(_mla_proj_kernel, scale=scale),
        grid=(n // tm,),
        in_specs=[
            pl.BlockSpec((tm, d), tok),
            pl.BlockSpec((tm, LANES), pos),
            _resident((1, d)), _resident(wdkv.shape), _resident((1, MLA_KV_LORA)),
            _resident(wuk.shape), _resident(wuv.shape),
            _resident((1, d)), _resident(wdq.shape), _resident((1, MLA_Q_LORA)),
            _resident(wuqn.shape), _resident(wuqr.shape),
        ],
        out_specs=[
            pl.BlockSpec((tm, hw), tok), pl.BlockSpec((tm, hw), tok), pl.BlockSpec((tm, hw), tok),
            pl.BlockSpec((tm, LANES), tok),
            pl.BlockSpec((tm // tk, hw, tk), lambda i: (i, 0, 0)),
        ],
        out_shape=[wide, wide, wide, jax.ShapeDtypeStruct((n, LANES), BF16),
                   jax.ShapeDtypeStruct((n // tk, hw, tk), BF16)],
        compiler_params=_params(("arbitrary",)),
        name="mla_proj",
    )(h, cs, gkv, wdkv, gc, wuk, wuv, gq, wdq, gcq, wuqn, wuqr)


def _mla_attn_kernel(qn_ref, qr_ref, kn_ref, kr_ref, vt_ref, o_ref, m_ref, acc_ref, gap_ref):
    tq = qn_ref.shape[0]
    tk = vt_ref.shape[-1]
    assert tq == tk
    heads = qn_ref.shape[1] // LANES
    strips = tq // ATTN_Q_STRIP
    n_chains = heads * strips
    qi = pl.program_id(2)
    ones = jnp.ones((SUM_ROWS, tk), BF16)

    def chain(c):
        g, st = divmod(c, strips)
        return g, slice(g * LANES, (g + 1) * LANES), slice(st * ATTN_Q_STRIP, (st + 1) * ATTN_Q_STRIP)

    def reset():
        m_ref[...] = jnp.full_like(m_ref, -jnp.inf)
        acc_ref[...] = jnp.zeros_like(acc_ref)

    def scores(c, j, nk):
        _, sl, cols = chain(c)
        rows = pl.ds(pl.multiple_of(j * tk, tk), nk)
        k = jnp.concatenate([kn_ref[rows, sl], kr_ref[rows, :]], axis=-1)
        q = jnp.concatenate([qn_ref[cols, sl], qr_ref[cols, sl]], axis=-1)
        return _dot_nt(k, q)

    def step(j, diagonal, lagged):
        def tile_scores(c):
            s = scores(c, j, (c % strips + 1) * ATTN_Q_STRIP if diagonal else tk)
            if diagonal:
                full = s.shape[0] - ATTN_Q_STRIP
                blk = (ATTN_Q_STRIP, ATTN_Q_STRIP)
                visible = lax.broadcasted_iota(jnp.int32, blk, 0) <= lax.broadcasted_iota(jnp.int32, blk, 1)
                last = jnp.where(visible, s[full:], -jnp.inf)
                s = jnp.concatenate([s[:full], last], axis=0) if full else last
            return s

        def softmax(c, s):
            g, _, cols = chain(c)
            m_prev = m_ref[g, :, cols]
            s_max = jnp.max(s, axis=0, keepdims=True)
            if lagged:
                m_prev = jnp.where(j == 0, s[0:1, :], m_prev)
                gap_ref[g, :, cols] = jnp.maximum(gap_ref[g, :, cols], s_max - m_prev)
            m_new = jnp.maximum(m_prev, s_max)
            m_ref[g, :, cols] = m_new
            p = jnp.exp2(s - (m_prev if lagged else m_new)).astype(BF16)
            return jnp.exp2(m_prev - m_new), p

        def accumulate(c, alpha, p):
            g, sl, cols = chain(c)
            nk = p.shape[0]
            vt1 = jnp.concatenate([vt_ref[j, sl, :nk], ones[:, :nk]], axis=0)
            pv = _dot(vt1, p)
            if lagged:
                acc_ref[g, :, cols] = alpha * (acc_ref[g, :, cols] + pv)
            else:
                acc_ref[g, :, cols] = alpha * acc_ref[g, :, cols] + pv

        s_q, p_q = {}, {}
        for t in range(n_chains + ATTN_LOOKAHEAD + ATTN_PV_DELAY):
            if t < n_chains:
                s_q[t] = tile_scores(t)
            c = t - ATTN_LOOKAHEAD
            if 0 <= c < n_chains:
                p_q[c] = softmax(c, s_q.pop(c))
            c = t - ATTN_LOOKAHEAD - ATTN_PV_DELAY
            if 0 <= c < n_chains:
                accumulate(c, *p_q.pop(c))

    def sweep(lagged):
        lax.fori_loop(0, qi, lambda j, c: (step(j, False, lagged), c)[1], 0)
        step(qi, True, lagged)

    reset()
    gap_ref[...] = jnp.zeros_like(gap_ref)
    sweep(lagged=True)

    @pl.when(jnp.max(gap_ref[...]) > ATTN_LAG_LIMIT)
    def _():
        reset()
        sweep(lagged=False)

    for g in range(heads):
        acc = acc_ref[g]
        o = acc[:MLA_V, :] / acc[MLA_V:MLA_V + 1, :]
        o_ref[:, g * LANES:(g + 1) * LANES] = o.T.astype(BF16)


def _mla_attn(qn, qr, kn, kr, vt, batch, seq):
    tq = _tiles()["attn_q"]
    tk = vt.shape[-1]
    g = ATTN_HEADS_PER_STEP
    hw = MLA_HEADS * LANES
    qn, qr, kn = (a.reshape(batch, seq, hw) for a in (qn, qr, kn))
    vt = vt.reshape(batch, seq // tk, hw, tk)
    kr = kr.reshape(batch, seq, LANES)
    q_blk = pl.BlockSpec((None, tq, g * LANES), lambda b, h, i: (b, i, h))
    kn_blk = pl.BlockSpec((None, seq, g * LANES), lambda b, h, i: (b, 0, h))
    kr_blk = pl.BlockSpec((None, seq, LANES), lambda b, h, i: (b, 0, 0))
    vt_blk = pl.BlockSpec((None, seq // tk, g * LANES, tk), lambda b, h, i: (b, 0, h, 0))
    out = pl.pallas_call(
        _mla_attn_kernel,
        grid=(batch, MLA_HEADS // g, seq // tq),
        in_specs=[q_blk, q_blk, kn_blk, kr_blk, vt_blk],
        out_specs=q_blk,
        out_shape=jax.ShapeDtypeStruct((batch, seq, hw), BF16),
        scratch_shapes=[
            pltpu.VMEM((g, 1, tq), F32), pltpu.VMEM((g, MLA_V + SUM_ROWS, tq), F32),
            pltpu.VMEM((g, 1, tq), F32),
        ],
        compiler_params=_params(("arbitrary", "arbitrary", "arbitrary")),
        name="mla_attn",
    )(qn, qr, kn, kr, vt)
    return out.reshape(batch * seq, hw)


def _rope_partner(w):
    half = MLA_ROPE // 2
    return jnp.concatenate([-w[..., half:], w[..., :half]], axis=-1)


def _rope_table(seq):
    half = MLA_ROPE // 2
    inv_freq = ROPE_THETA ** (-jnp.arange(half, dtype=F32) / half)
    ang = jnp.arange(seq, dtype=F32)[:, None] * inv_freq[None, :]
    cos, sin = jnp.cos(ang), jnp.sin(ang)
    return jnp.concatenate([cos, cos, sin, sin], axis=-1)


def kernel(x, hgrn_norm, hgrn_w_q, hgrn_w_f, hgrn_w_i, hgrn_w_g, hgrn_g_norm, hgrn_w_o, hgrn_lb_logits, mla_norm, mla_w_dq, mla_q_norm, mla_w_uq, mla_w_o, kv_in_norm, kv_w_dkv, kv_norm, kv_w_uk, kv_w_uv, mlp_norm, mlp_w_up, mlp_w_down, final_norm):
    batch, seq, d = x.shape
    assert d == D_MODEL and hgrn_w_q.shape[0] == 1 and mla_w_dq.shape[0] == 1
    assert seq % max(_tiles().values()) == 0
    n = batch * seq
    x2 = x.reshape(n, d)
    row = lambda g: g.reshape(1, -1).astype(F32)

    w4 = jnp.stack([hgrn_w_q[0], hgrn_w_f[0], hgrn_w_i[0], hgrn_w_g[0]]).astype(BF16)
    q, f, v, gate = _hgrn_proj(x2, row(hgrn_norm[0]), hgrn_lb_logits.astype(F32), w4)
    og = _hgrn_scan(q, f, v, gate, row(hgrn_g_norm[0]), batch, seq)
    h = _resid_mlp(x2, og, hgrn_w_o[0].astype(BF16), row(mlp_norm[0]),
                   mlp_w_up[0].astype(BF16), mlp_w_down[0].astype(BF16), row(final_norm), False)

    w_rope = kv_w_dkv[:, MLA_KV_LORA:]
    wdkv = jnp.concatenate([kv_w_dkv, _rope_partner(w_rope)], axis=-1).astype(BF16)
    wuq = mla_w_uq[0].reshape(MLA_Q_LORA, MLA_HEADS, MLA_NOPE + MLA_ROPE)
    wuqn = wuq[..., :MLA_NOPE].reshape(MLA_Q_LORA, MLA_HEADS * MLA_NOPE).astype(BF16)
    wq_rope = wuq[..., MLA_NOPE:]
    wuqr = jnp.concatenate([wq_rope, _rope_partner(wq_rope)], axis=-1)
    wuqr = wuqr.reshape(MLA_Q_LORA, MLA_HEADS * LANES).astype(BF16)
    qn, qr, kn, kr, vv = _mla_proj(
        h, _rope_table(seq), row(kv_in_norm), wdkv, row(kv_norm),
        kv_w_uk.astype(BF16), kv_w_uv.T.astype(BF16),
        row(mla_norm[0]), mla_w_dq[0].astype(BF16), row(mla_q_norm[0]), wuqn, wuqr, seq)

    attn = _mla_attn(qn, qr, kn, kr, vv, batch, seq)
    out = _resid_mlp(h, attn, mla_w_o[0].astype(BF16), row(mlp_norm[1]),
                     mlp_w_up[1].astype(BF16), mlp_w_down[1].astype(BF16), row(final_norm), True)
    return out.reshape(batch, seq, d)
```

```python
import functools
import math

import jax
import jax.numpy as jnp
from jax import lax
from jax.experimental import pallas as pl
from jax.experimental.pallas import tpu as pltpu

F32 = jnp.float32
BF16 = jnp.bfloat16

D_MODEL = 1024
HGRN_HEADS = 8
HGRN_DK = 128
MLA_HEADS = 16
MLA_NOPE = 128
MLA_ROPE = 64
MLA_V = 128
MLA_Q_LORA = 256
MLA_KV_LORA = 256
ROPE_THETA = 10000.0
D_FF = 4 * D_MODEL
EPS = 1e-6

LANES = 128
VMEM_LIMIT_BYTES = 56 * 1024 * 1024

PROJ_SUBTILES = 2
SCAN_CHUNK = 256
ATTN_HEADS_PER_STEP = 8
ATTN_Q_STRIP = 256
ATTN_LOOKAHEAD = 4
ATTN_PV_DELAY = 2
ATTN_LAG_LIMIT = 60.0
SUM_ROWS = 16


def _tiles():
    return dict(proj=512, scan=4096, mlp=512, mla_proj=512, attn_q=512, attn_k=512, ff=1024)


def _rms(x, gain):
    return x * lax.rsqrt(jnp.mean(x * x, axis=-1, keepdims=True) + EPS) * gain


def _sigmoid(x):
    return 1.0 / (1.0 + jnp.exp(-x))


def _dot(a, b):
    return jnp.dot(a, b, preferred_element_type=F32)


def _dot_nt(a, b):
    return lax.dot_general(a, b, (((1,), (1,)), ((), ())), preferred_element_type=F32)


def _dot_tn(a, b):
    return lax.dot_general(a, b, (((0,), (0,)), ((), ())), preferred_element_type=F32)


def _resident(shape):
    nd = len(shape)
    return pl.BlockSpec(shape, lambda *_: (0,) * nd, pipeline_mode=pl.Buffered(1))


def _params(semantics):
    return pltpu.CompilerParams(dimension_semantics=semantics, vmem_limit_bytes=VMEM_LIMIT_BYTES)


def _hgrn_proj_kernel(x_ref, gain_ref, lb_ref, w_ref, q_ref, f_ref, v_ref, g_ref):
    logits = lb_ref[...]
    e = jnp.exp(logits - jnp.max(logits, axis=0, keepdims=True))
    lb = e[0:1, :] / jnp.sum(e, axis=0, keepdims=True)
    sub = x_ref.shape[0] // PROJ_SUBTILES
    starts = range(0, x_ref.shape[0], sub)
    normed = [_rms(x_ref[r0:r0 + sub, :], gain_ref[...]).astype(BF16) for r0 in starts]
    for r0, xb in zip(starts, normed):
        rows = slice(r0, r0 + sub)
        yq = _dot(xb, w_ref[0])
        q_ref[rows, :] = (yq * _sigmoid(yq)).astype(BF16)
        f_ref[rows, :] = lb + (1.0 - lb) * _sigmoid(_dot(xb, w_ref[1]))
        yg = _dot(xb, w_ref[3])
        g_ref[rows, :] = (yg * _sigmoid(yg)).astype(BF16)
        v_ref[rows, :] = _dot(xb, w_ref[2]).astype(BF16)


def _hgrn_proj(x2, gain, lb_logits, w4):
    n, d = x2.shape
    tm = _tiles()["proj"]
    tok = lambda i: (i, 0)
    return pl.pallas_call(
        _hgrn_proj_kernel,
        grid=(n // tm,),
        in_specs=[
            pl.BlockSpec((tm, d), tok),
            _resident((1, d)),
            _resident(lb_logits.shape),
            _resident(w4.shape),
        ],
        out_specs=[pl.BlockSpec((tm, d), tok)] * 4,
        out_shape=[
            jax.ShapeDtypeStruct((n, d), BF16),
            jax.ShapeDtypeStruct((n, d), F32),
            jax.ShapeDtypeStruct((n, d), BF16),
            jax.ShapeDtypeStruct((n, d), BF16),
        ],
        compiler_params=_params(("arbitrary",)),
        name="hgrn_proj",
    )(x2, gain, lb_logits, w4)


SUBLANES = 8


def _scan_pack_lhs(x):
    half = x.shape[0] // 2
    return jnp.concatenate([x[:half], x[half:]], axis=1)


def _scan_pack_rhs(x):
    half = x.shape[0] // 2
    z = jnp.zeros((half, x.shape[1]), x.dtype)
    return jnp.concatenate([jnp.concatenate([x[:half], z], axis=1),
                            jnp.concatenate([z, x[half:]], axis=1)], axis=0)


def _scan_pair_dot(lhs, rhs):
    return _dot_nt(_scan_pack_lhs(lhs), _scan_pack_rhs(rhs))


def _hgrn_scan_kernel(q_ref, f_ref, v_ref, g_ref, gn_ref, o_ref, st_ref):
    c = SCAN_CHUNK
    half = c // 2
    nv = c // SUBLANES
    n_chunks = q_ref.shape[0] // c

    @pl.when(pl.program_id(2) == 0)
    def _():
        st_ref[...] = jnp.zeros_like(st_ref)

    t_i = lax.broadcasted_iota(jnp.int32, (half, c), 0)
    s_i = lax.broadcasted_iota(jnp.int32, (half, c), 1) & (half - 1)
    diff = t_i ^ s_i
    level = jnp.zeros((half, c), jnp.int32)
    h = 1
    while h < half:
        level = level + (diff >= h).astype(jnp.int32)
        h *= 2
    level = jnp.where(s_i <= t_i, level, -1)
    sub = lax.broadcasted_iota(jnp.int32, (SUBLANES, LANES), 0)
    zero = jnp.zeros((SUBLANES, LANES), F32)
    gn = gn_ref[...]

    def tiles(x):
        return [x[SUBLANES * a:SUBLANES * (a + 1)] for a in range(nv)]

    def cat(pieces):
        return jnp.concatenate(pieces, axis=0).astype(BF16)

    def levels(i):
        rows = pl.ds(i * c, c)
        f = f_ref[rows, :]
        qb = q_ref[rows, :]
        vb = v_ref[rows, :]
        fv = tiles(f)
        kv = [1.0 - x for x in fv]
        qv = tiles(qb.astype(F32))

        sc = jnp.where(level == 0, _scan_pair_dot(qb, cat(kv)), 0.0)
        lvl = 1
        g_pre = list(fv)
        h_suf = [None] * nv
        tot = list(fv)

        h = 1
        while h < SUBLANES:
            qt = [qv[a] * g_pre[a] for a in range(nv)]
            kt = [kv[a] if h_suf[a] is None else kv[a] * h_suf[a] for a in range(nv)]
            sc = jnp.where(level == lvl, _scan_pair_dot(cat(qt), cat(kt)), sc)
            lvl += 1
            upper = (sub & h) != 0
            for a in range(nv):
                up = pltpu.roll(tot[a], h, axis=0)
                dn = up if 2 * h == SUBLANES else pltpu.roll(tot[a], SUBLANES - h, axis=0)
                g_pre[a] = g_pre[a] * jnp.where(upper, up, 1.0)
                hs = jnp.where(upper, 1.0, dn)
                h_suf[a] = hs if h_suf[a] is None else h_suf[a] * hs
                tot[a] = tot[a] * (up if 2 * h == SUBLANES else jnp.where(upper, up, dn))
            h *= 2

        p_top = None
        m = 1
        while m < nv:
            is_upper = [(a // m) % 2 == 1 for a in range(nv)]
            qt = [qv[a] * g_pre[a] if is_upper[a] else zero for a in range(nv)]
            kt = [zero if is_upper[a] else kv[a] * h_suf[a] for a in range(nv)]
            if 4 * m < nv:
                sc = jnp.where(level == lvl, _scan_pair_dot(cat(qt), cat(kt)), sc)
                lvl += 1
            elif 2 * m < nv:
                sc = sc + _scan_pair_dot(cat(qt), cat(kt))
                lvl += 1
            else:
                p_top = _dot_nt(cat(qt[nv // 2:]), cat(kt[:nv // 2]))
            for b in range(0, nv, 2 * m):
                t_lo, t_hi = tot[b], tot[b + m]
                for a in range(b + m, b + 2 * m):
                    g_pre[a] = g_pre[a] * t_lo
                for a in range(b, b + m):
                    h_suf[a] = h_suf[a] * t_hi
                t_all = t_lo * t_hi
                for a in range(b, b + 2 * m):
                    tot[a] = t_all
            m *= 2

        q_all = cat([qv[a] * g_pre[a] for a in range(nv)])
        k_all = cat([kv[a] * h_suf[a] for a in range(nv)])
        return rows, vb, sc, p_top, q_all, _dot_tn(vb, k_all), tot[0][0:1, :]

    def finish(rows, vb, sc, p_top, q_all, st_in, decay):
        d = sc.astype(BF16)
        o_lo = _dot(d[:, :half], vb[:half])
        o_hi = _dot(jnp.concatenate([p_top.astype(BF16), d[:, half:]], axis=1), vb)

        st = st_ref[...]
        stb = st.astype(BF16)
        o_st = _scan_pair_dot(q_all, jnp.concatenate([stb, stb], axis=0))
        o = jnp.concatenate([o_st[:, :LANES] + o_lo, o_st[:, LANES:] + o_hi], axis=0)
        st_ref[...] = st * decay + st_in

        on = _rms(o, gn)
        o_ref[rows, :] = (on * g_ref[rows, :].astype(F32)).astype(BF16)

    pending = levels(0)
    for i in range(n_chunks):
        nxt = levels(i + 1) if i + 1 < n_chunks else None
        finish(*pending)
        pending = nxt


def _hgrn_scan(q, f, v, gate, g_norm, batch, seq):
    d = q.shape[-1]
    ts = _tiles()["scan"]
    q, f, v, gate = (a.reshape(batch, seq, d) for a in (q, f, v, gate))
    blk = pl.BlockSpec((None, ts, HGRN_DK), lambda b, h, s: (b, s, h))
    out = pl.pallas_call(
        _hgrn_scan_kernel,
        grid=(batch, HGRN_HEADS, seq // ts),
        in_specs=[blk, blk, blk, blk, _resident((1, HGRN_DK))],
        out_specs=blk,
        out_shape=jax.ShapeDtypeStruct((batch, seq, d), BF16),
        scratch_shapes=[pltpu.VMEM((HGRN_DK, HGRN_DK), F32)],
        compiler_params=_params(("arbitrary", "arbitrary", "arbitrary")),
        name="hgrn_scan",
    )(q, f, v, gate, g_norm)
    return out.reshape(batch * seq, d)


def _resid_mlp_kernel(h_ref, a_ref, wo_ref, gain_ref, wup_ref, wdn_ref, fin_ref, o_ref, *, final_norm):
    tf = _tiles()["ff"]
    h1 = h_ref[...] + _dot(a_ref[...], wo_ref[...])
    xb = _rms(h1, gain_ref[...]).astype(BF16)
    acc = h1
    for c0 in range(0, D_FF, tf):
        up = jnp.maximum(_dot(xb, wup_ref[:, c0:c0 + tf]), 0.0)
        acc = acc + _dot((up * up).astype(BF16), wdn_ref[c0:c0 + tf, :])
    if final_norm:
        acc = _rms(acc, fin_ref[...])
    o_ref[...] = acc


def _resid_mlp(h, a, wo, gain, wup, wdn, fin, final_norm):
    n, d = h.shape
    ka = a.shape[1]
    tm = _tiles()["mlp"]
    tok = lambda i: (i, 0)
    return pl.pallas_call(
        functools.partial(_resid_mlp_kernel, final_norm=final_norm),
        grid=(n // tm,),
        in_specs=[
            pl.BlockSpec((tm, d), tok),
            pl.BlockSpec((tm, ka), tok),
            _resident(wo.shape),
            _resident((1, d)),
            _resident(wup.shape),
            _resident(wdn.shape),
            _resident((1, d)),
        ],
        out_specs=pl.BlockSpec((tm, d), tok),
        out_shape=jax.ShapeDtypeStruct((n, d), F32),
        compiler_params=_params(("arbitrary",)),
        name="resid_mlp_final" if final_norm else "resid_mlp",
    )(h, a, wo, gain, wup, wdn, fin)


def _mla_proj_kernel(h_ref, cs_ref, gkv_ref, wdkv_ref, gc_ref, wuk_ref, wuv_ref,
                     gq_ref, wdq_ref, gcq_ref, wuqn_ref, wuqr_ref,
                     qn_ref, qr_ref, kn_ref, kr_ref, vt_ref, *, scale):
    tf = _tiles()["ff"]
    tk = vt_ref.shape[-1]
    sub = h_ref.shape[0] // PROJ_SUBTILES
    assert tk % sub == 0
    lane = lax.broadcasted_iota(jnp.int32, (sub, LANES), 1)
    def down(r0):
        h = h_ref[r0:r0 + sub, :]
        hn = h * lax.rsqrt(jnp.mean(h * h, axis=-1, keepdims=True) + EPS)
        return (_dot((hn * gkv_ref[...]).astype(BF16), wdkv_ref[...]),
                _dot((hn * gq_ref[...]).astype(BF16), wdq_ref[...]))

    starts = range(0, h_ref.shape[0], sub)
    latents = [down(r0) for r0 in starts]
    for r0, (ckr, cq) in zip(starts, latents):
        rows = slice(r0, r0 + sub)
        cs = cs_ref[rows, :]
        ckv = _rms(ckr[:, :MLA_KV_LORA], gc_ref[...]).astype(BF16)
        cq = _rms(cq, gcq_ref[...]).astype(BF16)
        qr = _dot(cq, wuqr_ref[...])
        for hd in range(MLA_HEADS):
            sl = slice(hd * LANES, (hd + 1) * LANES)
            t = qr[:, sl] * cs
            qr_ref[rows, sl] = ((t + pltpu.roll(t, MLA_ROPE, axis=1)) * scale).astype(BF16)
        t = ckr[:, MLA_KV_LORA:] * cs
        kr = t + pltpu.roll(t, MLA_ROPE, axis=1)
        kr_ref[rows, :] = jnp.where(lane < MLA_ROPE, kr, 0.0).astype(BF16)
        kn_ref[rows, :] = _dot(ckv, wuk_ref[...]).astype(BF16)
        kb, c0 = divmod(r0, tk)
        for f0 in range(0, vt_ref.shape[1], tf):
            vt_ref[kb, f0:f0 + tf, c0:c0 + sub] = _dot_nt(wuv_ref[f0:f0 + tf, :], ckv).astype(BF16)
        qn_ref[rows, :] = (_dot(cq, wuqn_ref[...]) * scale).astype(BF16)


def _mla_proj(h, cs, gkv, wdkv, gc, wuk, wuv, gq, wdq, gcq, wuqn, wuqr, seq):
    n, d = h.shape
    tm = _tiles()["mla_proj"]
    tk = _tiles()["attn_k"]
    hw = MLA_HEADS * LANES
    tok = lambda i: (i, 0)
    pos = lambda i: (i % (seq // tm), 0)
    scale = float((MLA_NOPE + MLA_ROPE) ** -0.5 * math.log2(math.e))
    wide = jax.ShapeDtypeStruct((n, hw), BF16)
    return pl.pallas_call(
        functools.partial(_mla_proj_kernel, scale=scale),
        grid=(n // tm,),
        in_specs=[
            pl.BlockSpec((tm, d), tok),
            pl.BlockSpec((tm, LANES), pos),
            _resident((1, d)), _resident(wdkv.shape), _resident((1, MLA_KV_LORA)),
            _resident(wuk.shape), _resident(wuv.shape),
            _resident((1, d)), _resident(wdq.shape), _resident((1, MLA_Q_LORA)),
            _resident(wuqn.shape), _resident(wuqr.shape),
        ],
        out_specs=[
            pl.BlockSpec((tm, hw), tok), pl.BlockSpec((tm, hw), tok), pl.BlockSpec((tm, hw), tok),
            pl.BlockSpec((tm, LANES), tok),
            pl.BlockSpec((tm // tk, hw, tk), lambda i: (i, 0, 0)),
        ],
        out_shape=[wide, wide, wide, jax.ShapeDtypeStruct((n, LANES), BF16),
                   jax.ShapeDtypeStruct((n // tk, hw, tk), BF16)],
        compiler_params=_params(("arbitrary",)),
        name="mla_proj",
    )(h, cs, gkv, wdkv, gc, wuk, wuv, gq, wdq, gcq, wuqn, wuqr)


def _mla_attn_kernel(qn_ref, qr_ref, kn_ref, kr_ref, vt_ref, o_ref, m_ref, acc_ref, gap_ref):
    tq = qn_ref.shape[0]
    tk = vt_ref.shape[-1]
    assert tq == tk
    heads = qn_ref.shape[1] // LANES
    strips = tq // ATTN_Q_STRIP
    n_chains = heads * strips
    qi = pl.program_id(2)
    ones = jnp.ones((SUM_ROWS, tk), BF16)

    def chain(c):
        g, st = divmod(c, strips)
        return g, slice(g * LANES, (g + 1) * LANES), slice(st * ATTN_Q_STRIP, (st + 1) * ATTN_Q_STRIP)

    def reset():
        m_ref[...] = jnp.full_like(m_ref, -jnp.inf)
        acc_ref[...] = jnp.zeros_like(acc_ref)

    def scores(c, j, nk):
        _, sl, cols = chain(c)
        rows = pl.ds(pl.multiple_of(j * tk, tk), nk)
        k = jnp.concatenate([kn_ref[rows, sl], kr_ref[rows, :]], axis=-1)
        q = jnp.concatenate([qn_ref[cols, sl], qr_ref[cols, sl]], axis=-1)
        return _dot_nt(k, q)

    def step(j, diagonal, lagged):
        def tile_scores(c):
            s = scores(c, j, (c % strips + 1) * ATTN_Q_STRIP if diagonal else tk)
            if diagonal:
                full = s.shape[0] - ATTN_Q_STRIP
                blk = (ATTN_Q_STRIP, ATTN_Q_STRIP)
                visible = lax.broadcasted_iota(jnp.int32, blk, 0) <= lax.broadcasted_iota(jnp.int32, blk, 1)
                last = jnp.where(visible, s[full:], -jnp.inf)
                s = jnp.concatenate([s[:full], last], axis=0) if full else last
            return s

        def softmax(c, s):
            g, _, cols = chain(c)
            m_prev = m_ref[g, :, cols]
            s_max = jnp.max(s, axis=0, keepdims=True)
            if lagged:
                m_prev = jnp.where(j == 0, s[0:1, :], m_prev)
                gap_ref[g, :, cols] = jnp.maximum(gap_ref[g, :, cols], s_max - m_prev)
            m_new = jnp.maximum(m_prev, s_max)
            m_ref[g, :, cols] = m_new
            p = jnp.exp2(s - (m_prev if lagged else m_new)).astype(BF16)
            return jnp.exp2(m_prev - m_new), p

        def accumulate(c, alpha, p):
            g, sl, cols = chain(c)
            nk = p.shape[0]
            vt1 = jnp.concatenate([vt_ref[j, sl, :nk], ones[:, :nk]], axis=0)
            pv = _dot(vt1, p)
            if lagged:
                acc_ref[g, :, cols] = alpha * (acc_ref[g, :, cols] + pv)
            else:
                acc_ref[g, :, cols] = alpha * acc_ref[g, :, cols] + pv

        s_q, p_q = {}, {}
        for t in range(n_chains + ATTN_LOOKAHEAD + ATTN_PV_DELAY):
            if t < n_chains:
                s_q[t] = tile_scores(t)
            c = t - ATTN_LOOKAHEAD
            if 0 <= c < n_chains:
                p_q[c] = softmax(c, s_q.pop(c))
            c = t - ATTN_LOOKAHEAD - ATTN_PV_DELAY
            if 0 <= c < n_chains:
                accumulate(c, *p_q.pop(c))

    def sweep(lagged):
        lax.fori_loop(0, qi, lambda j, c: (step(j, False, lagged), c)[1], 0)
        step(qi, True, lagged)

    reset()
    gap_ref[...] = jnp.zeros_like(gap_ref)
    sweep(lagged=True)

    @pl.when(jnp.max(gap_ref[...]) > ATTN_LAG_LIMIT)
    def _():
        reset()
        sweep(lagged=False)

    for g in range(heads):
        acc = acc_ref[g]
        o = acc[:MLA_V, :] / acc[MLA_V:MLA_V + 1, :]
        o_ref[:, g * LANES:(g + 1) * LANES] = o.T.astype(BF16)


def _mla_attn(qn, qr, kn, kr, vt, batch, seq):
    tq = _tiles()["attn_q"]
    tk = vt.shape[-1]
    g = ATTN_HEADS_PER_STEP
    hw = MLA_HEADS * LANES
    qn, qr, kn = (a.reshape(batch, seq, hw) for a in (qn, qr, kn))
    vt = vt.reshape(batch, seq // tk, hw, tk)
    kr = kr.reshape(batch, seq, LANES)
    q_blk = pl.BlockSpec((None, tq, g * LANES), lambda b, h, i: (b, i, h))
    kn_blk = pl.BlockSpec((None, seq, g * LANES), lambda b, h, i: (b, 0, h))
    kr_blk = pl.BlockSpec((None, seq, LANES), lambda b, h, i: (b, 0, 0))
    vt_blk = pl.BlockSpec((None, seq // tk, g * LANES, tk), lambda b, h, i: (b, 0, h, 0))
    out = pl.pallas_call(
        _mla_attn_kernel,
        grid=(batch, MLA_HEADS // g, seq // tq),
        in_specs=[q_blk, q_blk, kn_blk, kr_blk, vt_blk],
        out_specs=q_blk,
        out_shape=jax.ShapeDtypeStruct((batch, seq, hw), BF16),
        scratch_shapes=[
            pltpu.VMEM((g, 1, tq), F32), pltpu.VMEM((g, MLA_V + SUM_ROWS, tq), F32),
            pltpu.VMEM((g, 1, tq), F32),
        ],
        compiler_params=_params(("arbitrary", "arbitrary", "arbitrary")),
        name="mla_attn",
    )(qn, qr, kn, kr, vt)
    return out.reshape(batch * seq, hw)


def _rope_partner(w):
    half = MLA_ROPE // 2
    return jnp.concatenate([-w[..., half:], w[..., :half]], axis=-1)


def _rope_table(seq):
    half = MLA_ROPE // 2
    inv_freq = ROPE_THETA ** (-jnp.arange(half, dtype=F32) / half)
    ang = jnp.arange(seq, dtype=F32)[:, None] * inv_freq[None, :]
    cos, sin = jnp.cos(ang), jnp.sin(ang)
    return jnp.concatenate([cos, cos, sin, sin], axis=-1)


def kernel(x, hgrn_norm, hgrn_w_q, hgrn_w_f, hgrn_w_i, hgrn_w_g, hgrn_g_norm, hgrn_w_o, hgrn_lb_logits, mla_norm, mla_w_dq, mla_q_norm, mla_w_uq, mla_w_o, kv_in_norm, kv_w_dkv, kv_norm, kv_w_uk, kv_w_uv, mlp_norm, mlp_w_up, mlp_w_down, final_norm):
    batch, seq, d = x.shape
    assert d == D_MODEL and hgrn_w_q.shape[0] == 1 and mla_w_dq.shape[0] == 1
    assert seq % max(_tiles().values()) == 0
    n = batch * seq
    x2 = x.reshape(n, d)
    row = lambda g: g.reshape(1, -1).astype(F32)

    w4 = jnp.stack([hgrn_w_q[0], hgrn_w_f[0], hgrn_w_i[0], hgrn_w_g[0]]).astype(BF16)
    q, f, v, gate = _hgrn_proj(x2, row(hgrn_norm[0]), hgrn_lb_logits.astype(F32), w4)
    og = _hgrn_scan(q, f, v, gate, row(hgrn_g_norm[0]), batch, seq)
    h = _resid_mlp(x2, og, hgrn_w_o[0].astype(BF16), row(mlp_norm[0]),
                   mlp_w_up[0].astype(BF16), mlp_w_down[0].astype(BF16), row(final_norm), False)

    w_rope = kv_w_dkv[:, MLA_KV_LORA:]
    wdkv = jnp.concatenate([kv_w_dkv, _rope_partner(w_rope)], axis=-1).astype(BF16)
    wuq = mla_w_uq[0].reshape(MLA_Q_LORA, MLA_HEADS, MLA_NOPE + MLA_ROPE)
    wuqn = wuq[..., :MLA_NOPE].reshape(MLA_Q_LORA, MLA_HEADS * MLA_NOPE).astype(BF16)
    wq_rope = wuq[..., MLA_NOPE:]
    wuqr = jnp.concatenate([wq_rope, _rope_partner(wq_rope)], axis=-1)
    wuqr = wuqr.reshape(MLA_Q_LORA, MLA_HEADS * LANES).astype(BF16)
    qn, qr, kn, kr, vv = _mla_proj(
        h, _rope_table(seq), row(kv_in_norm), wdkv, row(kv_norm),
        kv_w_uk.astype(BF16), kv_w_uv.T.astype(BF16),
        row(mla_norm[0]), mla_w_dq[0].astype(BF16), row(mla_q_norm[0]), wuqn, wuqr, seq)

    attn = _mla_attn(qn, qr, kn, kr, vv, batch, seq)
    out = _resid_mlp(h, attn, mla_w_o[0].astype(BF16), row(mlp_norm[1]),
                     mlp_w_up[1].astype(BF16), mlp_w_down[1].astype(BF16), row(final_norm), True)
    return out.reshape(batch, seq, d)
```

```python
import functools
import math

import jax
import jax.numpy as jnp
from jax import lax
from jax.experimental import pallas as pl
from jax.experimental.pallas import tpu as pltpu

F32 = jnp.float32
BF16 = jnp.bfloat16

D_MODEL = 1024
HGRN_HEADS = 8
HGRN_DK = 128
MLA_HEADS = 16
MLA_NOPE = 128
MLA_ROPE = 64
MLA_V = 128
MLA_Q_LORA = 256
MLA_KV_LORA = 256
ROPE_THETA = 10000.0
D_FF = 4 * D_MODEL
EPS = 1e-6

LANES = 128
VMEM_LIMIT_BYTES = 56 * 1024 * 1024

HGRN_PROJ_SUBTILE = 256
MLA_PROJ_SUBTILE = 512
SCAN_CHUNK = 128
ATTN_HEADS_PER_STEP = 8
ATTN_Q_STRIP = 256
ATTN_LOOKAHEAD = 4
ATTN_PV_DELAY = 2
ATTN_LAG_LIMIT = 60.0
SUM_ROWS = 16


def _tiles():
    return dict(proj=1024, scan=4096, mlp=512, mla_proj=1024, attn_q=512, attn_k=512, ff=1024)


def _rms(x, gain):
    return x * lax.rsqrt(jnp.mean(x * x, axis=-1, keepdims=True) + EPS) * gain


def _sigmoid(x):
    return 1.0 / (1.0 + jnp.exp(-x))


def _dot(a, b):
    return jnp.dot(a, b, preferred_element_type=F32)


def _dot_nt(a, b):
    return lax.dot_general(a, b, (((1,), (1,)), ((), ())), preferred_element_type=F32)


def _dot_tn(a, b):
    return lax.dot_general(a, b, (((0,), (0,)), ((), ())), preferred_element_type=F32)


def _resident(shape):
    nd = len(shape)
    return pl.BlockSpec(shape, lambda *_: (0,) * nd, pipeline_mode=pl.Buffered(1))


def _params(semantics):
    return pltpu.CompilerParams(dimension_semantics=semantics, vmem_limit_bytes=VMEM_LIMIT_BYTES)


def _hgrn_proj_kernel(x_ref, gain_ref, lb_ref, w_ref, q_ref, f_ref, v_ref, g_ref):
    logits = lb_ref[...]
    e = jnp.exp(logits - jnp.max(logits, axis=0, keepdims=True))
    lb = e[0:1, :] / jnp.sum(e, axis=0, keepdims=True)
    sub = HGRN_PROJ_SUBTILE
    starts = range(0, x_ref.shape[0], sub)
    normed = [_rms(x_ref[r0:r0 + sub, :], gain_ref[...]).astype(BF16) for r0 in starts]
    for r0, xb in zip(starts, normed):
        rows = slice(r0, r0 + sub)
        yq = _dot(xb, w_ref[0])
        q_ref[rows, :] = (yq * _sigmoid(yq)).astype(BF16)
        f_ref[rows, :] = lb + (1.0 - lb) * _sigmoid(_dot(xb, w_ref[1]))
        yg = _dot(xb, w_ref[3])
        g_ref[rows, :] = (yg * _sigmoid(yg)).astype(BF16)
        v_ref[rows, :] = _dot(xb, w_ref[2]).astype(BF16)


def _hgrn_proj(x2, gain, lb_logits, w4):
    n, d = x2.shape
    tm = _tiles()["proj"]
    tok = lambda i: (i, 0)
    return pl.pallas_call(
        _hgrn_proj_kernel,
        grid=(n // tm,),
        in_specs=[
            pl.BlockSpec((tm, d), tok),
            _resident((1, d)),
            _resident(lb_logits.shape),
            _resident(w4.shape),
        ],
        out_specs=[pl.BlockSpec((tm, d), tok)] * 4,
        out_shape=[
            jax.ShapeDtypeStruct((n, d), BF16),
            jax.ShapeDtypeStruct((n, d), F32),
            jax.ShapeDtypeStruct((n, d), BF16),
            jax.ShapeDtypeStruct((n, d), BF16),
        ],
        compiler_params=_params(("arbitrary",)),
        name="hgrn_proj",
    )(x2, gain, lb_logits, w4)


SUBLANES = 8


def _scan_pack_lhs(x):
    half = x.shape[0] // 2
    return jnp.concatenate([x[:half], x[half:]], axis=1)


def _scan_pack_rhs(x):
    half = x.shape[0] // 2
    z = jnp.zeros((half, x.shape[1]), x.dtype)
    return jnp.concatenate([jnp.concatenate([x[:half], z], axis=1),
                            jnp.concatenate([z, x[half:]], axis=1)], axis=0)


def _scan_pair_dot(lhs, rhs):
    return _dot_nt(_scan_pack_lhs(lhs), _scan_pack_rhs(rhs))


def _hgrn_scan_kernel(q_ref, f_ref, v_ref, g_ref, gn_ref, o_ref, st_ref):
    c = SCAN_CHUNK
    half = c // 2
    nv = c // SUBLANES
    n_chunks = q_ref.shape[0] // c

    @pl.when(pl.program_id(2) == 0)
    def _():
        st_ref[...] = jnp.zeros_like(st_ref)

    t_i = lax.broadcasted_iota(jnp.int32, (half, c), 0)
    s_i = lax.broadcasted_iota(jnp.int32, (half, c), 1) & (half - 1)
    diff = t_i ^ s_i
    level = jnp.zeros((half, c), jnp.int32)
    h = 1
    while h < half:
        level = level + (diff >= h).astype(jnp.int32)
        h *= 2
    level = jnp.where(s_i <= t_i, level, -1)
    sub = lax.broadcasted_iota(jnp.int32, (SUBLANES, LANES), 0)
    zero = jnp.zeros((SUBLANES, LANES), F32)
    gn = gn_ref[...]

    def tiles(x):
        return [x[SUBLANES * a:SUBLANES * (a + 1)] for a in range(nv)]

    def cat(pieces):
        return jnp.concatenate(pieces, axis=0).astype(BF16)

    def levels(i):
        rows = pl.ds(i * c, c)
        f = f_ref[rows, :]
        qb = q_ref[rows, :]
        vb = v_ref[rows, :]
        fv = tiles(f)
        kv = [1.0 - x for x in fv]
        qv = tiles(qb.astype(F32))

        sc = jnp.where(level == 0, _scan_pair_dot(qb, cat(kv)), 0.0)
        lvl = 1
        g_pre = list(fv)
        h_suf = [None] * nv
        tot = list(fv)

        h = 1
        while h < SUBLANES:
            qt = [qv[a] * g_pre[a] for a in range(nv)]
            kt = [kv[a] if h_suf[a] is None else kv[a] * h_suf[a] for a in range(nv)]
            sc = jnp.where(level == lvl, _scan_pair_dot(cat(qt), cat(kt)), sc)
            lvl += 1
            upper = (sub & h) != 0
            for a in range(nv):
                up = pltpu.roll(tot[a], h, axis=0)
                dn = up if 2 * h == SUBLANES else pltpu.roll(tot[a], SUBLANES - h, axis=0)
                g_pre[a] = g_pre[a] * jnp.where(upper, up, 1.0)
                hs = jnp.where(upper, 1.0, dn)
                h_suf[a] = hs if h_suf[a] is None else h_suf[a] * hs
                tot[a] = tot[a] * (up if 2 * h == SUBLANES else jnp.where(upper, up, dn))
            h *= 2

        p_top = None
        m = 1
        while m < nv:
            is_upper = [(a // m) % 2 == 1 for a in range(nv)]
            qt = [qv[a] * g_pre[a] if is_upper[a] else zero for a in range(nv)]
            kt = [zero if is_upper[a] else kv[a] * h_suf[a] for a in range(nv)]
            if 4 * m < nv:
                sc = jnp.where(level == lvl, _scan_pair_dot(cat(qt), cat(kt)), sc)
                lvl += 1
            elif 2 * m < nv:
                sc = sc + _scan_pair_dot(cat(qt), cat(kt))
                lvl += 1
            else:
                p_top = _dot_nt(cat(qt[nv // 2:]), cat(kt[:nv // 2]))
            for b in range(0, nv, 2 * m):
                t_lo, t_hi = tot[b], tot[b + m]
                for a in range(b + m, b + 2 * m):
                    g_pre[a] = g_pre[a] * t_lo
                for a in range(b, b + m):
                    h_suf[a] = h_suf[a] * t_hi
                t_all = t_lo * t_hi
                for a in range(b, b + 2 * m):
                    tot[a] = t_all
            m *= 2

        q_all = cat([qv[a] * g_pre[a] for a in range(nv)])
        k_all = cat([kv[a] * h_suf[a] for a in range(nv)])
        return rows, vb, sc, p_top, q_all, _dot_tn(vb, k_all), tot[0][0:1, :]

    def finish(rows, vb, sc, p_top, q_all, st_in, decay):
        d = sc.astype(BF16)
        o_lo = _dot(d[:, :half], vb[:half])
        o_hi = _dot(jnp.concatenate([p_top.astype(BF16), d[:, half:]], axis=1), vb)

        st = st_ref[...]
        stb = st.astype(BF16)
        o_st = _scan_pair_dot(q_all, jnp.concatenate([stb, stb], axis=0))
        o = jnp.concatenate([o_st[:, :LANES] + o_lo, o_st[:, LANES:] + o_hi], axis=0)
        st_ref[...] = st * decay + st_in

        on = _rms(o, gn)
        o_ref[rows, :] = (on * g_ref[rows, :].astype(F32)).astype(BF16)

    pending = levels(0)
    for i in range(n_chunks):
        nxt = levels(i + 1) if i + 1 < n_chunks else None
        finish(*pending)
        pending = nxt


def _hgrn_scan(q, f, v, gate, g_norm, batch, seq):
    d = q.shape[-1]
    ts = _tiles()["scan"]
    q, f, v, gate = (a.reshape(batch, seq, d) for a in (q, f, v, gate))
    blk = pl.BlockSpec((None, ts, HGRN_DK), lambda b, h, s: (b, s, h))
    out = pl.pallas_call(
        _hgrn_scan_kernel,
        grid=(batch, HGRN_HEADS, seq // ts),
        in_specs=[blk, blk, blk, blk, _resident((1, HGRN_DK))],
        out_specs=blk,
        out_shape=jax.ShapeDtypeStruct((batch, seq, d), BF16),
        scratch_shapes=[pltpu.VMEM((HGRN_DK, HGRN_DK), F32)],
        compiler_params=_params(("arbitrary", "arbitrary", "arbitrary")),
        name="hgrn_scan",
    )(q, f, v, gate, g_norm)
    return out.reshape(batch * seq, d)


def _resid_mlp_kernel(h_ref, a_ref, wo_ref, gain_ref, wup_ref, wdn_ref, fin_ref, o_ref, *, final_norm):
    tf = _tiles()["ff"]
    h1 = h_ref[...] + _dot(a_ref[...], wo_ref[...])
    xb = _rms(h1, gain_ref[...]).astype(BF16)
    acc = h1
    for c0 in range(0, D_FF, tf):
        up = jnp.maximum(_dot(xb, wup_ref[:, c0:c0 + tf]), 0.0)
        acc = acc + _dot((up * up).astype(BF16), wdn_ref[c0:c0 + tf, :])
    if final_norm:
        acc = _rms(acc, fin_ref[...])
    o_ref[...] = acc


def _resid_mlp(h, a, wo, gain, wup, wdn, fin, final_norm):
    n, d = h.shape
    ka = a.shape[1]
    tm = _tiles()["mlp"]
    tok = lambda i: (i, 0)
    return pl.pallas_call(
        functools.partial(_resid_mlp_kernel, final_norm=final_norm),
        grid=(n // tm,),
        in_specs=[
            pl.BlockSpec((tm, d), tok),
            pl.BlockSpec((tm, ka), tok),
            _resident(wo.shape),
            _resident((1, d)),
            _resident(wup.shape),
            _resident(wdn.shape),
            _resident((1, d)),
        ],
        out_specs=pl.BlockSpec((tm, d), tok),
        out_shape=jax.ShapeDtypeStruct((n, d), F32),
        compiler_params=_params(("arbitrary",)),
        name="resid_mlp_final" if final_norm else "resid_mlp",
    )(h, a, wo, gain, wup, wdn, fin)


def _mla_proj_kernel(h_ref, cs_ref, gkv_ref, wdkv_ref, gc_ref, wuk_ref, wuv_ref,
                     gq_ref, wdq_ref, gcq_ref, wuqn_ref, wuqr_ref,
                     qn_ref, qr_ref, kn_ref, kr_ref, vt_ref, *, scale):
    tf = _tiles()["ff"]
    tk = vt_ref.shape[-1]
    sub = MLA_PROJ_SUBTILE
    assert tk % sub == 0 and h_ref.shape[0] % sub == 0
    lane = lax.broadcasted_iota(jnp.int32, (sub, LANES), 1)
    def down(r0):
        h = h_ref[r0:r0 + sub, :]
        hn = h * lax.rsqrt(jnp.mean(h * h, axis=-1, keepdims=True) + EPS)
        return (_dot((hn * gkv_ref[...]).astype(BF16), wdkv_ref[...]),
                _dot((hn * gq_ref[...]).astype(BF16), wdq_ref[...]))

    starts = range(0, h_ref.shape[0], sub)
    latents = [down(r0) for r0 in starts]
    for r0, (ckr, cq) in zip(starts, latents):
        rows = slice(r0, r0 + sub)
        cs = cs_ref[rows, :]
        ckv = _rms(ckr[:, :MLA_KV_LORA], gc_ref[...]).astype(BF16)
        cq = _rms(cq, gcq_ref[...]).astype(BF16)
        qr = _dot(cq, wuqr_ref[...])
        for hd in range(MLA_HEADS):
            sl = slice(hd * LANES, (hd + 1) * LANES)
            t = qr[:, sl] * cs
            qr_ref[rows, sl] = ((t + pltpu.roll(t, MLA_ROPE, axis=1)) * scale).astype(BF16)
        t = ckr[:, MLA_KV_LORA:] * cs
        kr = t + pltpu.roll(t, MLA_ROPE, axis=1)
        kr_ref[rows, :] = jnp.where(lane < MLA_ROPE, kr, 0.0).astype(BF16)
        kn_ref[rows, :] = _dot(ckv, wuk_ref[...]).astype(BF16)
        kb, c0 = divmod(r0, tk)
        for f0 in range(0, vt_ref.shape[1], tf):
            vt_ref[kb, f0:f0 + tf, c0:c0 + sub] = _dot_nt(wuv_ref[f0:f0 + tf, :], ckv).astype(BF16)
        qn_ref[rows, :] = (_dot(cq, wuqn_ref[...]) * scale).astype(BF16)


def _mla_proj(h, cs, gkv, wdkv, gc, wuk, wuv, gq, wdq, gcq, wuqn, wuqr, seq):
    n, d = h.shape
    tm = _tiles()["mla_proj"]
    tk = _tiles()["attn_k"]
    hw = MLA_HEADS * LANES
    tok = lambda i: (i, 0)
    pos = lambda i: (i % (seq // tm), 0)
    scale = float((MLA_NOPE + MLA_ROPE) ** -0.5 * math.log2(math.e))
    wide = jax.ShapeDtypeStruct((n, hw), BF16)
    return pl.pallas_call(
        functools.partial(_mla_proj_kernel, scale=scale),
        grid=(n // tm,),
        in_specs=[
            pl.BlockSpec((tm, d), tok),
            pl.BlockSpec((tm, LANES), pos),
            _resident((1, d)), _resident(wdkv.shape), _resident((1, MLA_KV_LORA)),
            _resident(wuk.shape), _resident(wuv.shape),
            _resident((1, d)), _resident(wdq.shape), _resident((1, MLA_Q_LORA)),
            _resident(wuqn.shape), _resident(wuqr.shape),
        ],
        out_specs=[
            pl.BlockSpec((tm, hw), tok), pl.BlockSpec((tm, hw), tok), pl.BlockSpec((tm, hw), tok),
            pl.BlockSpec((tm, LANES), tok),
            pl.BlockSpec((tm // tk, hw, tk), lambda i: (i, 0, 0)),
        ],
        out_shape=[wide, wide, wide, jax.ShapeDtypeStruct((n, LANES), BF16),
                   jax.ShapeDtypeStruct((n // tk, hw, tk), BF16)],
        compiler_params=_params(("arbitrary",)),
        name="mla_proj",
    )(h, cs, gkv, wdkv, gc, wuk, wuv, gq, wdq, gcq, wuqn, wuqr)


def _mla_attn_kernel(qn_ref, qr_ref, kn_ref, kr_ref, vt_ref, o_ref, m_ref, acc_ref, gap_ref):
    tq = qn_ref.shape[0]
    tk = vt_ref.shape[-1]
    assert tq == tk
    heads = qn_ref.shape[1] // LANES
    strips = tq // ATTN_Q_STRIP
    n_chains = heads * strips
    qi = pl.program_id(2)
    ones = jnp.ones((SUM_ROWS, tk), BF16)

    def chain(c):
        g, st = divmod(c, strips)
        return g, slice(g * LANES, (g + 1) * LANES), slice(st * ATTN_Q_STRIP, (st + 1) * ATTN_Q_STRIP)

    def reset():
        m_ref[...] = jnp.full_like(m_ref, -jnp.inf)
        acc_ref[...] = jnp.zeros_like(acc_ref)

    def scores(c, j, nk):
        _, sl, cols = chain(c)
        rows = pl.ds(pl.multiple_of(j * tk, tk), nk)
        k = jnp.concatenate([kn_ref[rows, sl], kr_ref[rows, :]], axis=-1)
        q = jnp.concatenate([qn_ref[cols, sl], qr_ref[cols, sl]], axis=-1)
        return _dot_nt(k, q)

    def step(j, diagonal, lagged):
        def tile_scores(c):
            s = scores(c, j, (c % strips + 1) * ATTN_Q_STRIP if diagonal else tk)
            if diagonal:
                full = s.shape[0] - ATTN_Q_STRIP
                blk = (ATTN_Q_STRIP, ATTN_Q_STRIP)
                visible = lax.broadcasted_iota(jnp.int32, blk, 0) <= lax.broadcasted_iota(jnp.int32, blk, 1)
                last = jnp.where(visible, s[full:], -jnp.inf)
                s = jnp.concatenate([s[:full], last], axis=0) if full else last
            return s

        def softmax(c, s):
            g, _, cols = chain(c)
            m_prev = m_ref[g, :, cols]
            s_max = jnp.max(s, axis=0, keepdims=True)
            if lagged:
                m_prev = jnp.where(j == 0, s[0:1, :], m_prev)
                gap_ref[g, :, cols] = jnp.maximum(gap_ref[g, :, cols], s_max - m_prev)
            m_new = jnp.maximum(m_prev, s_max)
            m_ref[g, :, cols] = m_new
            p = jnp.exp2(s - (m_prev if lagged else m_new)).astype(BF16)
            return jnp.exp2(m_prev - m_new), p

        def accumulate(c, alpha, p):
            g, sl, cols = chain(c)
            nk = p.shape[0]
            vt1 = jnp.concatenate([vt_ref[j, sl, :nk], ones[:, :nk]], axis=0)
            pv = _dot(vt1, p)
            if lagged:
                acc_ref[g, :, cols] = alpha * (acc_ref[g, :, cols] + pv)
            else:
                acc_ref[g, :, cols] = alpha * acc_ref[g, :, cols] + pv

        s_q, p_q = {}, {}
        for t in range(n_chains + ATTN_LOOKAHEAD + ATTN_PV_DELAY):
            if t < n_chains:
                s_q[t] = tile_scores(t)
            c = t - ATTN_LOOKAHEAD
            if 0 <= c < n_chains:
                p_q[c] = softmax(c, s_q.pop(c))
            c = t - ATTN_LOOKAHEAD - ATTN_PV_DELAY
            if 0 <= c < n_chains:
                accumulate(c, *p_q.pop(c))

    def sweep(lagged):
        lax.fori_loop(0, qi, lambda j, c: (step(j, False, lagged), c)[1], 0)
        step(qi, True, lagged)

    reset()
    gap_ref[...] = jnp.zeros_like(gap_ref)
    sweep(lagged=True)

    @pl.when(jnp.max(gap_ref[...]) > ATTN_LAG_LIMIT)
    def _():
        reset()
        sweep(lagged=False)

    for g in range(heads):
        acc = acc_ref[g]
        o = acc[:MLA_V, :] / acc[MLA_V:MLA_V + 1, :]
        o_ref[:, g * LANES:(g + 1) * LANES] = o.T.astype(BF16)


def _mla_attn(qn, qr, kn, kr, vt, batch, seq):
    tq = _tiles()["attn_q"]
    tk = vt.shape[-1]
    g = ATTN_HEADS_PER_STEP
    hw = MLA_HEADS * LANES
    qn, qr, kn = (a.reshape(batch, seq, hw) for a in (qn, qr, kn))
    vt = vt.reshape(batch, seq // tk, hw, tk)
    kr = kr.reshape(batch, seq, LANES)
    q_blk = pl.BlockSpec((None, tq, g * LANES), lambda b, h, i: (b, i, h))
    kn_blk = pl.BlockSpec((None, seq, g * LANES), lambda b, h, i: (b, 0, h))
    kr_blk = pl.BlockSpec((None, seq, LANES), lambda b, h, i: (b, 0, 0))
    vt_blk = pl.BlockSpec((None, seq // tk, g * LANES, tk), lambda b, h, i: (b, 0, h, 0))
    out = pl.pallas_call(
        _mla_attn_kernel,
        grid=(batch, MLA_HEADS // g, seq // tq),
        in_specs=[q_blk, q_blk, kn_blk, kr_blk, vt_blk],
        out_specs=q_blk,
        out_shape=jax.ShapeDtypeStruct((batch, seq, hw), BF16),
        scratch_shapes=[
            pltpu.VMEM((g, 1, tq), F32), pltpu.VMEM((g, MLA_V + SUM_ROWS, tq), F32),
            pltpu.VMEM((g, 1, tq), F32),
        ],
        compiler_params=_params(("arbitrary", "arbitrary", "arbitrary")),
        name="mla_attn",
    )(qn, qr, kn, kr, vt)
    return out.reshape(batch * seq, hw)


def _rope_partner(w):
    half = MLA_ROPE // 2
    return jnp.concatenate([-w[..., half:], w[..., :half]], axis=-1)


def _rope_table(seq):
    half = MLA_ROPE // 2
    inv_freq = ROPE_THETA ** (-jnp.arange(half, dtype=F32) / half)
    ang = jnp.arange(seq, dtype=F32)[:, None] * inv_freq[None, :]
    cos, sin = jnp.cos(ang), jnp.sin(ang)
    return jnp.concatenate([cos, cos, sin, sin], axis=-1)


def kernel(x, hgrn_norm, hgrn_w_q, hgrn_w_f, hgrn_w_i, hgrn_w_g, hgrn_g_norm, hgrn_w_o, hgrn_lb_logits, mla_norm, mla_w_dq, mla_q_norm, mla_w_uq, mla_w_o, kv_in_norm, kv_w_dkv, kv_norm, kv_w_uk, kv_w_uv, mlp_norm, mlp_w_up, mlp_w_down, final_norm):
    batch, seq, d = x.shape
    assert d == D_MODEL and hgrn_w_q.shape[0] == 1 and mla_w_dq.shape[0] == 1
    assert seq % max(_tiles().values()) == 0
    n = batch * seq
    x2 = x.reshape(n, d)
    row = lambda g: g.reshape(1, -1).astype(F32)

    w4 = jnp.stack([hgrn_w_q[0], hgrn_w_f[0], hgrn_w_i[0], hgrn_w_g[0]]).astype(BF16)
    q, f, v, gate = _hgrn_proj(x2, row(hgrn_norm[0]), hgrn_lb_logits.astype(F32), w4)
    og = _hgrn_scan(q, f, v, gate, row(hgrn_g_norm[0]), batch, seq)
    h = _resid_mlp(x2, og, hgrn_w_o[0].astype(BF16), row(mlp_norm[0]),
                   mlp_w_up[0].astype(BF16), mlp_w_down[0].astype(BF16), row(final_norm), False)

    w_rope = kv_w_dkv[:, MLA_KV_LORA:]
    wdkv = jnp.concatenate([kv_w_dkv, _rope_partner(w_rope)], axis=-1).astype(BF16)
    wuq = mla_w_uq[0].reshape(MLA_Q_LORA, MLA_HEADS, MLA_NOPE + MLA_ROPE)
    wuqn = wuq[..., :MLA_NOPE].reshape(MLA_Q_LORA, MLA_HEADS * MLA_NOPE).astype(BF16)
    wq_rope = wuq[..., MLA_NOPE:]
    wuqr = jnp.concatenate([wq_rope, _rope_partner(wq_rope)], axis=-1)
    wuqr = wuqr.reshape(MLA_Q_LORA, MLA_HEADS * LANES).astype(BF16)
    qn, qr, kn, kr, vv = _mla_proj(
        h, _rope_table(seq), row(kv_in_norm), wdkv, row(kv_norm),
        kv_w_uk.astype(BF16), kv_w_uv.T.astype(BF16),
        row(mla_norm[0]), mla_w_dq[0].astype(BF16), row(mla_q_norm[0]), wuqn, wuqr, seq)

    attn = _mla_attn(qn, qr, kn, kr, vv, batch, seq)
    out = _resid_mlp(h, attn, mla_w_o[0].astype(BF16), row(mlp_norm[1]),
                     mlp_w_up[1].astype(BF16), mlp_w_down[1].astype(BF16), row(final_norm), True)
    return out.reshape(batch, seq, d)
```
